```python
import math
import jax, jax.numpy as jnp
from jax import lax
import numpy as np

D_MODEL = 1024
BATCH = 1
SEQ = 16384
DEPTH = 2
DEC_BATCH = 32
DEC_SEQ = 4
PAST_LEN = 16384
PAGE_SIZE = 128

HEAD_DIM = 64
MIX_WIDTH = D_MODEL // 2
A_HEADS = MIX_WIDTH // (2 * HEAD_DIM)
A_VDIM = 2 * HEAD_DIM
B_HEADS = MIX_WIDTH // HEAD_DIM
B_KV_HEADS = max(1, B_HEADS // 4)
B_GROUP = B_HEADS // B_KV_HEADS
CMP_LEN = 32
CMP_STRIDE = 16
CMP_HIDDEN = HEAD_DIM
SEL_BLOCK = 64
SEL_TOPK = 16
WINDOW = 512
C_HEADS = MIX_WIDTH // HEAD_DIM
MOBA_BLOCK = 256
MOBA_TOPK = 3
D_HEADS = MIX_WIDTH // HEAD_DIM
N_BUCKETS = 32
MAX_EXACT = 16
MAX_DISTANCE = 128
N_BIAS_HEADS = max(A_HEADS + B_HEADS, C_HEADS)

QBLOCK = 128
NORM_EPS = 1e-6
NEG_INF = -1e30
SEL_FORCE = 1e4

A_WIDTH = A_HEADS * A_VDIM
B_WIDTH = B_HEADS * HEAD_DIM
C_WIDTH = C_HEADS * HEAD_DIM
D_WIDTH = D_HEADS * HEAD_DIM
AB_SPLITS = (2 * A_HEADS * HEAD_DIM, 2 * A_HEADS * HEAD_DIM, A_WIDTH, A_WIDTH,
             B_WIDTH, B_KV_HEADS * HEAD_DIM, B_KV_HEADS * HEAD_DIM, B_KV_HEADS * HEAD_DIM,
             B_KV_HEADS * HEAD_DIM, B_KV_HEADS * HEAD_DIM, B_KV_HEADS * HEAD_DIM, 3 * B_HEADS, B_WIDTH)
CD_SPLITS = (C_WIDTH, C_WIDTH, C_WIDTH, C_WIDTH, D_WIDTH, D_WIDTH, D_WIDTH, D_WIDTH)

kernel_name = "hybrid_diff_nsa_moba_stickbreak_step"


def _split(x, widths):
    idx = [int(i) for i in np.cumsum(widths)[:-1]]
    return jnp.split(x, idx, axis=-1)


def _rmsnorm(x, g):
    xf = x.astype(jnp.float32)
    y = xf * lax.rsqrt(jnp.mean(xf * xf, axis=-1, keepdims=True) + NORM_EPS)
    return (y * g.astype(jnp.float32)).astype(x.dtype)


def _qblock(T):
    return QBLOCK if T % QBLOCK == 0 else T


def _sweep(fn, q_pos0, *q_arrays):
    B, T = q_arrays[0].shape[:2]
    qb = _qblock(T)
    nb = T // qb
    blocks = tuple(jnp.swapaxes(a.reshape((B, nb, qb) + a.shape[2:]), 0, 1) for a in q_arrays)
    starts = q_pos0 + qb * jnp.arange(nb, dtype=jnp.int32)
    out = lax.map(lambda args: fn(args[0], qb, *args[1:]), (starts,) + blocks)
    out = jnp.swapaxes(out, 0, 1)
    return out.reshape((B, T) + out.shape[3:])


def _bucket(dist):
    n = jnp.maximum(dist, 0)
    nf = jnp.maximum(n, 1).astype(jnp.float32)
    large = MAX_EXACT + (jnp.log(nf / MAX_EXACT) / math.log(MAX_DISTANCE / MAX_EXACT)
                         * (N_BUCKETS - MAX_EXACT)).astype(jnp.int32)
    return jnp.where(n < MAX_EXACT, n, jnp.minimum(large, N_BUCKETS - 1))


def _masked_softmax(s, mask):
    s = jnp.where(mask, s.astype(jnp.float32), NEG_INF)
    m = jnp.max(s, axis=-1, keepdims=True)
    p = jnp.where(mask, jnp.exp(s - m), 0.0)
    return p / jnp.maximum(jnp.sum(p, axis=-1, keepdims=True), 1e-30)


def _diff_attention(q, k, v, lam, lam_init, head_gain, tbl):
    B, T = q.shape[:2]
    L = k.shape[1]
    k1, k2 = k[..., :HEAD_DIM], k[..., HEAD_DIM:]
    kpos = jnp.arange(L, dtype=jnp.int32)
    scale = HEAD_DIM ** -0.5
    tblf = tbl.astype(jnp.float32)

    def block(start, qb, qblk):
        qpos = start + jnp.arange(qb, dtype=jnp.int32)
        dist = qpos[:, None] - kpos[None, :]
        mask = (dist >= 0)[None, None]
        bias = jnp.moveaxis(tblf[_bucket(dist)], -1, 0)[None]
        s1 = jnp.einsum('bqhd,bkhd->bhqk', qblk[..., 0, :], k1).astype(jnp.float32) * scale + bias
        s2 = jnp.einsum('bqhd,bkhd->bhqk', qblk[..., 1, :], k2).astype(jnp.float32) * scale + bias
        p = _masked_softmax(s1, mask) - lam * _masked_softmax(s2, mask)
        return jnp.einsum('bhqk,bkhd->bqhd', p.astype(v.dtype), v)

    o = _sweep(block, L - T, q)
    return _rmsnorm(o, head_gain) * (1.0 - lam_init)


def _compress(x, pos, w1, w2):
    B, L = x.shape[:2]
    n_chunks = L // CMP_STRIDE
    ch = x[:, :n_chunks * CMP_STRIDE].reshape(B, n_chunks, CMP_STRIDE, B_KV_HEADS, HEAD_DIM)
    half = CMP_STRIDE * HEAD_DIM

    def flat(p):
        return jnp.moveaxis(ch + p[:, None, :], 2, 3).reshape(B, n_chunks, B_KV_HEADS, half)

    first = flat(pos[:CMP_STRIDE]) @ w1[:half]
    second = flat(pos[CMP_STRIDE:]) @ w1[half:]
    h = jax.nn.silu(first[:, :-1] + second[:, 1:])
    return h @ w2


def _nsa(q, kc, vc, ks, vs, kw, vw, gates, cmp_pos, cmp_w1, cmp_w2, tbl):
    B, T = q.shape[:2]
    L = kc.shape[1]
    Lw = kw.shape[1]
    scale = HEAD_DIM ** -0.5
    kcmp = _compress(kc, cmp_pos[0], cmp_w1[0], cmp_w2[0])
    vcmp = _compress(vc, cmp_pos[1], cmp_w1[1], cmp_w2[1])
    n_cmp = kcmp.shape[1]
    n_chunks = n_cmp + 1
    cmp_end = jnp.arange(n_cmp, dtype=jnp.int32) * CMP_STRIDE + (CMP_LEN - 1)
    n_sel = -(-L // SEL_BLOCK)
    pad = n_sel * SEL_BLOCK - L
    ks_b = jnp.pad(ks, ((0, 0), (0, pad), (0, 0), (0, 0))).reshape(B, n_sel, SEL_BLOCK, B_KV_HEADS, HEAD_DIM)
    vs_b = jnp.pad(vs, ((0, 0), (0, pad), (0, 0), (0, 0))).reshape(B, n_sel, SEL_BLOCK, B_KV_HEADS, HEAD_DIM)
    n_top = min(SEL_TOPK, n_sel)
    chunks_per_sel = SEL_BLOCK // CMP_STRIDE
    chunk_pad = n_sel * chunks_per_sel - n_chunks
    kw_p = jnp.pad(kw, ((0, 0), (WINDOW, 0), (0, 0), (0, 0)))
    vw_p = jnp.pad(vw, ((0, 0), (WINDOW, 0), (0, 0), (0, 0)))
    w_valid = jnp.arange(Lw + WINDOW) >= WINDOW
    w_pos0 = L - Lw
    tblf = tbl.astype(jnp.float32)
    tbl_g = jnp.transpose(tblf.reshape(N_BUCKETS, B_KV_HEADS, B_GROUP), (1, 0, 2))
    bi = jnp.arange(B)[:, None, None, None]
    hi = jnp.arange(B_KV_HEADS)[None, None, :, None]

    def group_bias(dist):
        b = tblf[_bucket(dist)].reshape(dist.shape + (B_KV_HEADS, B_GROUP))
        return jnp.moveaxis(b, 1, -1)[None]

    def block(start, qb, qblk, gblk):
        qpos = start + jnp.arange(qb, dtype=jnp.int32)
        qg = qblk.reshape(B, qb, B_KV_HEADS, B_GROUP, HEAD_DIM)
        dist_c = qpos[:, None] - cmp_end[None, :]
        s_c = jnp.einsum('bqhgd,bnhd->bqhgn', qg, kcmp).astype(jnp.float32) * scale + group_bias(dist_c)
        p_c = _masked_softmax(s_c, (dist_c >= 0)[None, :, None, None, :])
        o_c = jnp.einsum('bqhgn,bnhd->bqhgd', p_c.astype(vcmp.dtype), vcmp)
        imp = jnp.sum(p_c, axis=3)
        z1 = jnp.zeros(imp.shape[:-1] + (1,), jnp.float32)
        chunk = jnp.concatenate([imp, z1], -1) + jnp.concatenate([z1, imp], -1)
        chunk = jnp.pad(chunk, ((0, 0), (0, 0), (0, 0), (0, chunk_pad)))
        imp_sel = chunk.reshape(imp.shape[:-1] + (n_sel, chunks_per_sel)).sum(-1)
        blk = jnp.arange(n_sel, dtype=jnp.int32)[None, None, None, :]
        cur = (qpos // SEL_BLOCK)[None, :, None, None]
        valid = blk <= cur
        forced = (blk == 0) | (blk == cur) | (blk == cur - 1)
        score = jnp.where(valid, imp_sel + jnp.where(forced, SEL_FORCE, 0.0), NEG_INF)
        top_s, top_i = lax.top_k(score, n_top)
        ok = top_s > 0.5 * NEG_INF
        kg = ks_b[bi, top_i, :, hi].reshape(B, qb, B_KV_HEADS, n_top * SEL_BLOCK, HEAD_DIM)
        vg = vs_b[bi, top_i, :, hi].reshape(B, qb, B_KV_HEADS, n_top * SEL_BLOCK, HEAD_DIM)
        kpos_s = top_i[..., None] * SEL_BLOCK + jnp.arange(SEL_BLOCK, dtype=jnp.int32)
        dist_s = qpos[None, :, None, None, None] - kpos_s
        mask_s = (ok[..., None] & (dist_s >= 0)).reshape(B, qb, B_KV_HEADS, 1, n_top * SEL_BLOCK)
        bias_s = tbl_g[hi[..., None], _bucket(dist_s)]
        bias_s = jnp.moveaxis(bias_s.reshape(B, qb, B_KV_HEADS, n_top * SEL_BLOCK, B_GROUP), -1, 3)
        s_s = jnp.einsum('bqhgd,bqhkd->bqhgk', qg, kg).astype(jnp.float32) * scale + bias_s
        p_s = _masked_softmax(s_s, mask_s)
        o_s = jnp.einsum('bqhgk,bqhkd->bqhgd', p_s.astype(vg.dtype), vg)
        off = start - w_pos0
        kwb = lax.dynamic_slice_in_dim(kw_p, off, qb + WINDOW, axis=1)
        vwb = lax.dynamic_slice_in_dim(vw_p, off, qb + WINDOW, axis=1)
        validb = lax.dynamic_slice_in_dim(w_valid, off, qb + WINDOW)
        kpos_w = start - WINDOW + jnp.arange(qb + WINDOW, dtype=jnp.int32)
        dist_w = qpos[:, None] - kpos_w[None, :]
        mask_w = (validb[None, :] & (dist_w >= 0) & (dist_w <= WINDOW))[None, :, None, None, :]
        s_w = jnp.einsum('bqhgd,bkhd->bqhgk', qg, kwb).astype(jnp.float32) * scale + group_bias(dist_w)
        p_w = _masked_softmax(s_w, mask_w)
        o_w = jnp.einsum('bqhgk,bkhd->bqhgd', p_w.astype(vwb.dtype), vwb)
        g = gblk.reshape(B, qb, B_KV_HEADS, B_GROUP, 3).astype(o_c.dtype)
        o = g[..., 0:1] * o_c + g[..., 1:2] * o_s + g[..., 2:3] * o_w
        return o.reshape(B, qb, B_HEADS, HEAD_DIM)

    return _sweep(block, L - T, q, gates)


def _moba(q, k, v, tbl):
    B, T = q.shape[:2]
    L = k.shape[1]
    scale = HEAD_DIM ** -0.5
    qbs = _qblock(T)
    n_full = L // MOBA_BLOCK
    n_blk = -(-L // MOBA_BLOCK)
    pad = n_blk * MOBA_BLOCK + qbs - L
    kp = jnp.pad(k, ((0, 0), (0, pad), (0, 0), (0, 0)))
    vp = jnp.pad(v, ((0, 0), (0, pad), (0, 0), (0, 0)))
    k_blocks = kp[:, :n_blk * MOBA_BLOCK].reshape(B, n_blk, MOBA_BLOCK, C_HEADS, HEAD_DIM)
    v_blocks = vp[:, :n_blk * MOBA_BLOCK].reshape(B, n_blk, MOBA_BLOCK, C_HEADS, HEAD_DIM)
    kmean = jnp.mean(k_blocks[:, :n_full].astype(jnp.float32), axis=2)
    n_top = min(MOBA_TOPK, n_full)
    tblf = tbl.astype(jnp.float32)
    tblT = tblf.T
    bi = jnp.arange(B)[:, None, None, None]
    hi = jnp.arange(C_HEADS)[None, None, :, None]

    def block(start, qb, qblk):
        qpos = start + jnp.arange(qb, dtype=jnp.int32)
        cur = qpos // MOBA_BLOCK
        o0 = (start // MOBA_BLOCK) * MOBA_BLOCK
        ko = lax.dynamic_slice_in_dim(kp, o0, MOBA_BLOCK + qb, axis=1)
        vo = lax.dynamic_slice_in_dim(vp, o0, MOBA_BLOCK + qb, axis=1)
        kpos_o = o0 + jnp.arange(MOBA_BLOCK + qb, dtype=jnp.int32)
        dist_o = qpos[:, None] - kpos_o[None, :]
        mask_o = ((kpos_o[None, :] // MOBA_BLOCK) == cur[:, None]) & (dist_o >= 0)
        mask_o = jnp.broadcast_to(mask_o[None, :, None, :], (B, qb, C_HEADS, MOBA_BLOCK + qb))
        s_o = (jnp.einsum('bqhd,bkhd->bqhk', qblk, ko).astype(jnp.float32) * scale
               + jnp.moveaxis(tblf[_bucket(dist_o)], -1, 1)[None])
        if n_top == 0:
            p = _masked_softmax(s_o, mask_o)
            return jnp.einsum('bqhk,bkhd->bqhd', p.astype(vo.dtype), vo)
        gate = jnp.einsum('bqhd,bnhd->bqhn', qblk.astype(jnp.float32), kmean)
        past = (jnp.arange(n_full, dtype=jnp.int32)[None, :] < cur[:, None])[None, :, None, :]
        top_s, top_i = lax.top_k(jnp.where(past, gate, NEG_INF), n_top)
        ok = top_s > 0.5 * NEG_INF
        kg = k_blocks[bi, top_i, :, hi].reshape(B, qb, C_HEADS, n_top * MOBA_BLOCK, HEAD_DIM)
        vg = v_blocks[bi, top_i, :, hi].reshape(B, qb, C_HEADS, n_top * MOBA_BLOCK, HEAD_DIM)
        kpos_g = (top_i[..., None] * MOBA_BLOCK + jnp.arange(MOBA_BLOCK, dtype=jnp.int32)).reshape(
            B, qb, C_HEADS, n_top * MOBA_BLOCK)
        bias_g = tblT[hi, _bucket(qpos[None, :, None, None] - kpos_g)]
        s_g = jnp.einsum('bqhd,bqhkd->bqhk', qblk, kg).astype(jnp.float32) * scale + bias_g
        mask_g = jnp.repeat(ok, MOBA_BLOCK, axis=-1)
        p = _masked_softmax(jnp.concatenate([s_g, s_o], -1), jnp.concatenate([mask_g, mask_o], -1))
        ng = n_top * MOBA_BLOCK
        return (jnp.einsum('bqhk,bqhkd->bqhd', p[..., :ng].astype(vg.dtype), vg)
                + jnp.einsum('bqhk,bkhd->bqhd', p[..., ng:].astype(vo.dtype), vo))

    return _sweep(block, L - T, q)


def _stick_breaking(q, k, v):
    B, T = q.shape[:2]
    L = k.shape[1]
    kpos = jnp.arange(L, dtype=jnp.int32)
    scale = HEAD_DIM ** -0.5

    def block(start, qb, qblk):
        qpos = start + jnp.arange(qb, dtype=jnp.int32)
        mask = (kpos[None, :] < qpos[:, None])[None, None]
        z = jnp.einsum('bqhd,bkhd->bhqk', qblk, k).astype(jnp.float32) * scale
        log_1m = jnp.where(mask, jax.nn.log_sigmoid(-z), 0.0)
        after = lax.cumsum(log_1m, axis=3, reverse=True) - log_1m
        w = jnp.where(mask, jnp.exp(jax.nn.log_sigmoid(z) + after), 0.0)
        return jnp.einsum('bhqk,bkhd->bqhd', w.astype(v.dtype), v)

    return _sweep(block, L - T, q)


def _cat(past, new):
    if past is None:
        return list(new)
    return [jnp.concatenate([p, n], axis=1) for p, n in zip(past, new)]


def _layer_ab(x, past, norm_g, w_in, w_out, lam_p, head_gain, cmp_pos, cmp_w1, cmp_w2, rel_bias, li):
    B, T, _ = x.shape
    h = _rmsnorm(x, norm_g)
    (a_q, a_k, a_v, a_gate, b_q, b_kc, b_vc, b_ks, b_vs, b_kw, b_vw, b_g, b_gate) = _split(h @ w_in, AB_SPLITS)
    a_k = a_k.reshape(B, T, A_HEADS, 2 * HEAD_DIM)
    a_v = a_v.reshape(B, T, A_HEADS, A_VDIM)
    b_new = [t.reshape(B, T, B_KV_HEADS, HEAD_DIM) for t in (b_kc, b_vc, b_ks, b_vs, b_kw, b_vw)]
    full = _cat(past, [a_k, a_v] + b_new)
    lam_f = lam_p.astype(jnp.float32)
    lam_init = 0.8 - 0.6 * math.exp(-0.3 * li)
    lam = jnp.exp(jnp.dot(lam_f[0], lam_f[1])) - jnp.exp(jnp.dot(lam_f[2], lam_f[3])) + lam_init
    o_a = _diff_attention(a_q.reshape(B, T, A_HEADS, 2, HEAD_DIM), full[0], full[1], lam, lam_init,
                          head_gain, rel_bias[:, :A_HEADS])
    o_b = _nsa(b_q.reshape(B, T, B_HEADS, HEAD_DIM), full[2], full[3], full[4], full[5], full[6], full[7],
               jax.nn.sigmoid(b_g.reshape(B, T, B_HEADS, 3)), cmp_pos, cmp_w1, cmp_w2,
               rel_bias[:, A_HEADS:A_HEADS + B_HEADS])
    mixed = jnp.concatenate([o_a.reshape(B, T, A_WIDTH) * jax.nn.silu(a_gate),
                             o_b.reshape(B, T, B_WIDTH) * jax.nn.silu(b_gate)], axis=-1)
    y = x + mixed @ w_out
    keep = min(WINDOW, full[6].shape[1])
    win = jnp.stack([full[6][:, -keep:], full[7][:, -keep:]], axis=2)
    return y, jnp.stack([a_k, a_v], axis=2), jnp.stack(b_new[:4], axis=2), win


def _layer_cd(x, past, norm_g, w_in, w_out, rel_bias):
    B, T, _ = x.shape
    h = _rmsnorm(x, norm_g)
    c_q, c_k, c_v, c_gate, d_q, d_k, d_v, d_gate = _split(h @ w_in, CD_SPLITS)
    c_q, c_k, c_v = (t.reshape(B, T, C_HEADS, HEAD_DIM) for t in (c_q, c_k, c_v))
    d_q, d_k, d_v = (t.reshape(B, T, D_HEADS, HEAD_DIM) for t in (d_q, d_k, d_v))
    full = _cat(past, [c_k, c_v, d_k, d_v])
    o_c = _moba(c_q, full[0], full[1], rel_bias[:, :C_HEADS])
    o_d = _stick_breaking(d_q, full[2], full[3])
    mixed = jnp.concatenate([o_c.reshape(B, T, C_WIDTH) * jax.nn.silu(c_gate),
                             o_d.reshape(B, T, D_WIDTH) * jax.nn.silu(d_gate)], axis=-1)
    y = x + mixed @ w_out
    return y, jnp.stack([c_k, c_v], axis=2), jnp.stack([d_k, d_v], axis=2)


def _paged(pool, j, page_table, c):
    g = pool[j, page_table, :, c]
    return g.reshape((page_table.shape[0], -1) + g.shape[3:])


def setup_inputs(seed: int = 0) -> dict:
    key = jax.random.key(seed)
    ks = jax.random.split(key, 24)
    f32 = jnp.float32
    n_ab = (DEPTH + 1) // 2
    n_cd = DEPTH // 2
    n_pages = PAST_LEN // PAGE_SIZE
    n_used = DEC_BATCH * n_pages
    n_phys = n_used + (n_used + 3) // 4
    win_buf = min(WINDOW, PAST_LEN)

    def nrm(k, shape, scale=1.0):
        return jax.random.normal(k, shape, f32) * scale

    page_table = jax.random.permutation(ks[0], n_phys)[:n_used].reshape(DEC_BATCH, n_pages).astype(jnp.int32)
    ab_in = sum(AB_SPLITS)
    cd_in = sum(CD_SPLITS)
    return {
        "x_prompt": nrm(ks[1], (BATCH, SEQ, D_MODEL)),
        "x_sample": nrm(ks[2], (DEC_BATCH, DEC_SEQ, D_MODEL)),
        "cache_a_kv": nrm(ks[3], (n_ab, n_phys, PAGE_SIZE, 2, A_HEADS, 2 * HEAD_DIM)),
        "cache_b_kv": nrm(ks[4], (n_ab, n_phys, PAGE_SIZE, 4, B_KV_HEADS, HEAD_DIM)),
        "cache_b_win": nrm(ks[5], (n_ab, DEC_BATCH, win_buf, 2, B_KV_HEADS, HEAD_DIM)),
        "cache_c_kv": nrm(ks[6], (n_cd, n_phys, PAGE_SIZE, 2, C_HEADS, HEAD_DIM)),
        "cache_d_kv": nrm(ks[7], (n_cd, n_phys, PAGE_SIZE, 2, D_HEADS, HEAD_DIM)),
        "page_table": page_table,
        "rel_bias": nrm(ks[8], (N_BUCKETS, N_BIAS_HEADS), 0.5),
        "ab_norm": 1.0 + nrm(ks[9], (n_ab, D_MODEL), 0.02),
        "ab_w_in": nrm(ks[10], (n_ab, D_MODEL, ab_in), D_MODEL ** -0.5),
        "ab_w_out": nrm(ks[11], (n_ab, A_WIDTH + B_WIDTH, D_MODEL), (A_WIDTH + B_WIDTH) ** -0.5),
        "ab_lambda": nrm(ks[12], (n_ab, 4, HEAD_DIM), 0.1),
        "ab_head_norm": 1.0 + nrm(ks[13], (n_ab, A_VDIM), 0.02),
        "ab_cmp_pos": nrm(ks[14], (n_ab, 2, CMP_LEN, HEAD_DIM), 0.1),
        "ab_cmp_w1": nrm(ks[15], (n_ab, 2, CMP_LEN * HEAD_DIM, CMP_HIDDEN), (CMP_LEN * HEAD_DIM) ** -0.5),
        "ab_cmp_w2": nrm(ks[16], (n_ab, 2, CMP_HIDDEN, HEAD_DIM), CMP_HIDDEN ** -0.5),
        "cd_norm": 1.0 + nrm(ks[17], (n_cd, D_MODEL), 0.02),
        "cd_w_in": nrm(ks[18], (n_cd, D_MODEL, cd_in), D_MODEL ** -0.5),
        "cd_w_out": nrm(ks[19], (n_cd, C_WIDTH + D_WIDTH, D_MODEL), (C_WIDTH + D_WIDTH) ** -0.5),
        "final_norm": 1.0 + nrm(ks[20], (D_MODEL,), 0.02),
    }


def reference(x_prompt, x_sample, cache_a_kv, cache_b_kv, cache_b_win, cache_c_kv, cache_d_kv, page_table,
              rel_bias, ab_norm, ab_w_in, ab_w_out, ab_lambda, ab_head_norm, ab_cmp_pos, ab_cmp_w1, ab_cmp_w2,
              cd_norm, cd_w_in, cd_w_out, final_norm):
    yp, ys = x_prompt, x_sample
    a_p, a_s, b_p, b_s, w_p, w_s, c_p, c_s, d_p, d_s = [], [], [], [], [], [], [], [], [], []
    for li in range(DEPTH):
        j = li // 2
        if li % 2 == 0:
            wts = (ab_norm[j], ab_w_in[j], ab_w_out[j], ab_lambda[j], ab_head_norm[j],
                   ab_cmp_pos[j], ab_cmp_w1[j], ab_cmp_w2[j], rel_bias, li)
            yp, a_new, b_new, w_new = _layer_ab(yp, None, *wts)
            a_p.append(a_new); b_p.append(b_new); w_p.append(w_new)
            past = ([_paged(cache_a_kv, j, page_table, c) for c in range(2)]
                    + [_paged(cache_b_kv, j, page_table, c) for c in range(4)]
                    + [cache_b_win[j, :, :, 0], cache_b_win[j, :, :, 1]])
            ys, a_new, b_new, w_new = _layer_ab(ys, past, *wts)
            a_s.append(a_new); b_s.append(b_new); w_s.append(w_new)
        else:
            wts = (cd_norm[j], cd_w_in[j], cd_w_out[j], rel_bias)
            yp, c_new, d_new = _layer_cd(yp, None, *wts)
            c_p.append(c_new); d_p.append(d_new)
            past = ([_paged(cache_c_kv, j, page_table, c) for c in range(2)]
                    + [_paged(cache_d_kv, j, page_table, c) for c in range(2)])
            ys, c_new, d_new = _layer_cd(ys, past, *wts)
            c_s.append(c_new); d_s.append(d_new)
    y_prompt = _rmsnorm(yp, final_norm)
    y_sample = _rmsnorm(ys, final_norm)
    return (y_prompt, y_sample, jnp.stack(a_p), jnp.stack(a_s), jnp.stack(b_p), jnp.stack(b_s),
            jnp.stack(w_p), jnp.stack(w_s), jnp.stack(c_p), jnp.stack(c_s), jnp.stack(d_p), jnp.stack(d_s))
```

```python
import functools
import math

import numpy as np
import jax
import jax.numpy as jnp
from jax import lax
from jax.experimental import pallas as pl
from jax.experimental.pallas import tpu as pltpu

F32 = jnp.float32
BF16 = jnp.bfloat16
I32 = jnp.int32

D_MODEL = 1024
HEAD_DIM = 64
LANES = 128
A_HEADS = 4
B_HEADS = 8
B_KV_HEADS = 2
B_GROUP = 4
C_HEADS = 8
D_HEADS = 8
CMP_STRIDE = 16
SEL_BLOCK = 64
SEL_TOPK = 16
WINDOW = 512
MOBA_BLOCK = 256
MOBA_TOPK = 3
N_BUCKETS = 32
MAX_EXACT = 16
MAX_DISTANCE = 128
PAGE = 128
NORM_EPS = 1e-6
NEG_INF = -1e30
M_INIT = -3.0e38
SEL_FORCE = 1e4
SCALE = HEAD_DIM ** -0.5

TQ = 512
TK = 512
TM = 256
VMEM_LIMIT = 56 * 1024 * 1024


def _bucket_thresholds():
    n = np.arange(0, 4 * MAX_DISTANCE)
    nf = np.maximum(n, 1).astype(np.float64)
    large = MAX_EXACT + (np.log(nf / MAX_EXACT) / math.log(MAX_DISTANCE / MAX_EXACT)
                         * (N_BUCKETS - MAX_EXACT)).astype(np.int32)
    b = np.where(n < MAX_EXACT, n, np.minimum(large, N_BUCKETS - 1))
    return [int(np.argmax(b >= k)) for k in range(N_BUCKETS)]


_THR = _bucket_thresholds()


def _cparams(sem):
    return pltpu.CompilerParams(dimension_semantics=sem, vmem_limit_bytes=VMEM_LIMIT)


def _silu(x):
    return x * jax.nn.sigmoid(x)


def _dot_nt(a, b):
    return lax.dot_general(a, b, (((1,), (1,)), ((), ())), preferred_element_type=F32)


def _dot(a, b):
    return jnp.dot(a, b, preferred_element_type=F32)


def _split3(x):
    x1 = x.astype(BF16)
    r = x - x1.astype(F32)
    x2 = r.astype(BF16)
    x3 = (r - x2.astype(F32)).astype(BF16)
    return x1, x2, x3


def _bias_of_dist(dist, tbl_ref, h):
    val = jnp.full(dist.shape, tbl_ref[0, h], F32)
    for b in range(1, N_BUCKETS):
        val = jnp.where(dist >= _THR[b], tbl_ref[b, h], val)
    return jnp.where(dist >= 0, val - tbl_ref[N_BUCKETS - 1, h], 0.0)


def _bias_kernel(tbl_ref, o_ref, *, tk, tq):
    h = pl.program_id(0)
    d = pl.program_id(1)
    row = lax.broadcasted_iota(I32, (tk, tq), 0)
    col = lax.broadcasted_iota(I32, (tk, tq), 1)
    o_ref[...] = _bias_of_dist(d * tk + col - row, tbl_ref, h)


def _bias_tiles(rel_bias, tk, tq):
    nh = rel_bias.shape[1]
    return pl.pallas_call(
        functools.partial(_bias_kernel, tk=tk, tq=tq),
        grid=(nh, 2),
        in_specs=[pl.BlockSpec(memory_space=pltpu.SMEM)],
        out_specs=pl.BlockSpec((None, None, tk, tq), lambda h, d: (h, d, 0, 0)),
        out_shape=jax.ShapeDtypeStruct((nh, 2, tk, tq), F32),
        compiler_params=_cparams(("arbitrary", "arbitrary")),
        name="bias_tiles",
    )(rel_bias)


def _cmp_bias_kernel(tbl_ref, o_ref, *, nr, tq):
    h = pl.program_id(0)
    row = lax.broadcasted_iota(I32, (nr, tq), 0)
    col = lax.broadcasted_iota(I32, (nr, tq), 1)
    o_ref[...] = _bias_of_dist(col - CMP_STRIDE * row + (tq - 2 * CMP_STRIDE + 1), tbl_ref, h)


def _cmp_bias_tiles(rel_bias, nr, tq):
    nh = rel_bias.shape[1]
    return pl.pallas_call(
        functools.partial(_cmp_bias_kernel, nr=nr, tq=tq),
        grid=(nh,),
        in_specs=[pl.BlockSpec(memory_space=pltpu.SMEM)],
        out_specs=pl.BlockSpec((None, nr, tq), lambda h: (h, 0, 0)),
        out_shape=jax.ShapeDtypeStruct((nh, nr, tq), F32),
        compiler_params=_cparams(("arbitrary",)),
        name="cmp_bias_tiles",
    )(rel_bias)


def _proj_kernel(x_ref, g_ref, w_ref, *o_refs, outs, tm):
    i = pl.program_id(0)
    x = x_ref[...]
    h = x * lax.rsqrt(jnp.mean(x * x, axis=-1, keepdims=True) + NORM_EPS) * g_ref[...]
    hb = h.astype(BF16)
    lane = lax.broadcasted_iota(I32, (tm, LANES), 1)
    rowg = i * tm + lax.broadcasted_iota(I32, (tm, LANES), 0)
    for o_ref, (kind, c0, width, arg) in zip(o_refs, outs):
        r = _dot(hb, w_ref[:, c0:c0 + width])
        if kind == "f32":
            o_ref[...] = r
        elif kind == "silu":
            o_ref[...] = _silu(r)
        elif kind == "sigmoid":
            o_ref[...] = jax.nn.sigmoid(r)
        elif kind == "bf16":
            o_ref[...] = (r * arg).astype(BF16)
        elif kind == "vT":
            for b in range(width // LANES):
                o_ref[b] = r[:, b * LANES:(b + 1) * LANES].T.astype(BF16)
        elif kind == "kaug":
            onehot = jnp.where(lane - HEAD_DIM == (rowg // arg) % HEAD_DIM, 1.0, 0.0)
            for hh in range(width // HEAD_DIM):
                slab = r[:, (hh // 2) * LANES:(hh // 2 + 1) * LANES]
                if hh % 2:
                    slab = pltpu.roll(slab, HEAD_DIM, 1)
                o_ref[hh] = jnp.where(lane < HEAD_DIM, slab, onehot).astype(BF16)
        else:
            raise ValueError(kind)


def _project(x2d, gain, w, outs):
    t = x2d.shape[0]
    tm = min(TM, t)
    vt_w = min(TK, t)
    assert t % tm == 0 and vt_w % tm == 0
    out_shapes, out_specs = [], []
    for kind, c0, width, arg in outs:
        if kind in ("f32", "silu", "sigmoid"):
            out_shapes.append(jax.ShapeDtypeStruct((t, width), F32))
            out_specs.append(pl.BlockSpec((tm, width), lambda i: (i, 0)))
        elif kind == "bf16":
            out_shapes.append(jax.ShapeDtypeStruct((t, width), BF16))
            out_specs.append(pl.BlockSpec((tm, width), lambda i: (i, 0)))
        elif kind == "vT":
            nb = width // LANES
            per = vt_w // tm
            out_shapes.append(jax.ShapeDtypeStruct((nb, t // vt_w, LANES, vt_w), BF16))
            out_specs.append(pl.BlockSpec((nb, None, LANES, tm), lambda i, per=per: (0, i // per, 0, i % per)))
        elif kind == "kaug":
            nh = width // HEAD_DIM
            out_shapes.append(jax.ShapeDtypeStruct((nh, t, LANES), BF16))
            out_specs.append(pl.BlockSpec((nh, tm, LANES), lambda i: (0, i, 0)))
    return pl.pallas_call(
        functools.partial(_proj_kernel, outs=tuple(outs), tm=tm),
        grid=(t // tm,),
        in_specs=[pl.BlockSpec((tm, D_MODEL), lambda i: (i, 0)),
                  pl.BlockSpec((1, D_MODEL), lambda i: (0, 0)),
                  pl.BlockSpec(w.shape, lambda i: (0, 0))],
        out_specs=out_specs,
        out_shape=out_shapes,
        compiler_params=_cparams(("arbitrary",)),
        name="in_proj",
    )(x2d, gain.reshape(1, D_MODEL), w)


def _outproj_kernel(x_ref, o1_ref, g1_ref, o2_ref, g2_ref, w_ref, fg_ref, y_ref, *, final):
    mixed = jnp.concatenate([o1_ref[...] * g1_ref[...], o2_ref[...] * g2_ref[...]], axis=1)
    y = x_ref[...] + _dot(mixed.astype(BF16), w_ref[...])
    if final:
        y = y * lax.rsqrt(jnp.mean(y * y, axis=-1, keepdims=True) + NORM_EPS) * fg_ref[...]
    y_ref[...] = y


def _out_project(x2d, o1, g1, o2, g2, w_out, final_gain, final):
    t = x2d.shape[0]
    tm = min(TM, t)
    half = D_MODEL // 2
    return pl.pallas_call(
        functools.partial(_outproj_kernel, final=final),
        grid=(t // tm,),
        in_specs=[pl.BlockSpec((tm, D_MODEL), lambda i: (i, 0)),
                  pl.BlockSpec((tm, half), lambda i: (i, 0)),
                  pl.BlockSpec((tm, half), lambda i: (i, 0)),
                  pl.BlockSpec((tm, half), lambda i: (i, 0)),
                  pl.BlockSpec((tm, half), lambda i: (i, 0)),
                  pl.BlockSpec((D_MODEL, D_MODEL), lambda i: (0, 0)),
                  pl.BlockSpec((1, D_MODEL), lambda i: (0, 0))],
        out_specs=pl.BlockSpec((tm, D_MODEL), lambda i: (i, 0)),
        out_shape=jax.ShapeDtypeStruct((t, D_MODEL), F32),
        compiler_params=_cparams(("arbitrary",)),
        name="out_proj",
    )(x2d, o1, g1, o2, g2, w_out.astype(BF16), final_gain.reshape(1, D_MODEL))


def _osm_update(s, vt, m, l, acc):
    m_new = jnp.maximum(m, jnp.max(s, axis=0, keepdims=True))
    alpha = jnp.exp(m - m_new)
    p = jnp.exp(s - m_new)
    l = alpha * l + jnp.sum(p, axis=0, keepdims=True)
    acc = alpha * acc + _dot(vt, p.astype(BF16))
    return m_new, l, acc


def _osm_init(n):
    return (jnp.full((1, n), M_INIT, F32), jnp.zeros((1, n), F32), jnp.zeros((LANES, n), F32))


def _osm_finish(m, l, acc):
    inv = jnp.where(m > 0.5 * NEG_INF, 1.0 / jnp.maximum(l, 1e-30), 0.0)
    return acc * inv


def _tile_rows(ref, j, tk):
    return ref[pl.ds(pl.multiple_of(j * tk, tk), tk), :]


def _causal_tile(tk, n, tq):
    row = lax.broadcasted_iota(I32, (tk, n), 0)
    col = lax.broadcasted_iota(I32, (tk, n), 1) % tq
    return row <= col


def _pair_rows(acc_even, acc_odd):
    row = lax.broadcasted_iota(I32, acc_even.shape, 0)
    return jnp.where(row < HEAD_DIM, acc_even, acc_odd)


def _a_attn_kernel(lam_ref, q_ref, k_ref, vt_ref, bias_ref, gain_ref, o_ref, *, tq, tk, lam_init):
    i = pl.program_id(1)
    q = q_ref[...]
    lane = lax.broadcasted_iota(I32, (tq, LANES), 1)
    zero = jnp.zeros_like(q)
    qs = jnp.concatenate([jnp.where(lane < HEAD_DIM, q, zero), jnp.where(lane >= HEAD_DIM, q, zero)], axis=0)
    n = 2 * tq

    def far(j, carry):
        return _osm_update(_dot_nt(_tile_rows(k_ref, j, tk), qs), vt_ref[j], *carry)

    carry = lax.fori_loop(0, jnp.maximum(i - 1, 0), far, _osm_init(n))
    jp = jnp.maximum(i - 1, 0)
    b1 = bias_ref[1]
    s = _dot_nt(_tile_rows(k_ref, jp, tk), qs) + jnp.concatenate([b1, b1], axis=1)
    s = jnp.where(i >= 1, s, NEG_INF)
    carry = _osm_update(s, vt_ref[jp], *carry)
    b0 = bias_ref[0]
    s = _dot_nt(_tile_rows(k_ref, i, tk), qs) + jnp.concatenate([b0, b0], axis=1)
    s = jnp.where(_causal_tile(tk, n, tq), s, NEG_INF)
    m, l, acc = _osm_update(s, vt_ref[i], *carry)
    o2 = _osm_finish(m, l, acc)
    lp = lam_ref[...]
    lam = (jnp.exp(jnp.sum(lp[0:1] * lp[1:2], axis=1, keepdims=True))
           - jnp.exp(jnp.sum(lp[2:3] * lp[3:4], axis=1, keepdims=True)) + lam_init)
    o = (o2[:, :tq] - lam * o2[:, tq:]).T
    o = o * lax.rsqrt(jnp.mean(o * o, axis=-1, keepdims=True) + NORM_EPS)
    o_ref[...] = o * gain_ref[...] * (1.0 - lam_init)


def _a_attention(lam_p, qa, ka, vat, bias, gain, lam_init):
    t = qa.shape[0]
    tq = tk = min(TQ, t)
    nk = t // tk
    return pl.pallas_call(
        functools.partial(_a_attn_kernel, tq=tq, tk=tk, lam_init=lam_init),
        grid=(A_HEADS, t // tq),
        in_specs=[pl.BlockSpec(lam_p.shape, lambda h, i: (0, 0)),
                  pl.BlockSpec((tq, LANES), lambda h, i: (i, h)),
                  pl.BlockSpec((t, LANES), lambda h, i: (0, h)),
                  pl.BlockSpec((None, nk, LANES, tk), lambda h, i: (h, 0, 0, 0)),
                  pl.BlockSpec((None, 2, tk, tq), lambda h, i: (h, 0, 0, 0)),
                  pl.BlockSpec((1, LANES), lambda h, i: (0, 0))],
        out_specs=pl.BlockSpec((tq, LANES), lambda h, i: (i, h)),
        out_shape=jax.ShapeDtypeStruct((t, A_HEADS * LANES), F32),
        compiler_params=_cparams(("arbitrary", "arbitrary")),
        name="a_attn",
    )(lam_p, qa, ka, vat, bias, gain.reshape(1, LANES))


def _resident(shape, index_map):
    return pl.BlockSpec(shape, index_map, pipeline_mode=pl.Buffered(1))


def _d_attn_kernel(q_ref, k_ref, vt_ref, o_ref, *, tq, tk, sub):
    i = pl.program_id(1)
    q = q_ref[...]
    lane = lax.broadcasted_iota(I32, (tq, LANES), 1)
    zero = jnp.zeros_like(q)
    qs = jnp.concatenate([jnp.where(lane < HEAD_DIM, q, zero), jnp.where(lane >= HEAD_DIM, q, zero)], axis=0)
    n = 2 * tq
    ur = lax.broadcasted_iota(I32, (sub, sub), 0)
    uc = lax.broadcasted_iota(I32, (sub, sub), 1)
    upper = jnp.where(uc > ur, 1.0, 0.0).astype(BF16)
    row = lax.broadcasted_iota(I32, (sub, n), 0)
    col = lax.broadcasted_iota(I32, (sub, n), 1) % tq

    def sub_tile(j, hf, carry, diag):
        acc, run = carry
        k = k_ref[pl.ds(pl.multiple_of(j * tk + hf * sub, sub), sub), :]
        z = _dot_nt(k, qs)
        ll = -(jnp.maximum(z, 0.0) + jnp.log1p(jnp.exp(-jnp.abs(z))))
        if diag:
            mask = (row + hf * sub) < col
            ll = jnp.where(mask, ll, 0.0)
        lh = ll.astype(BF16)
        lo = (ll - lh.astype(F32)).astype(BF16)
        after = _dot(upper, lh) + _dot(upper, lo) + run
        w = jnp.exp(z + ll + after)
        if diag:
            w = jnp.where(mask, w, 0.0)
        acc = acc + _dot(vt_ref[j, :, hf * sub:(hf + 1) * sub], w.astype(BF16))
        run = run + jnp.sum(ll, axis=0, keepdims=True)
        return acc, run

    carry = (jnp.zeros((LANES, n), F32), jnp.zeros((1, n), F32))
    for hf in reversed(range(tk // sub)):
        carry = sub_tile(i, hf, carry, True)

    def far(jj, carry):
        j = i - 1 - jj
        for hf in reversed(range(tk // sub)):
            carry = sub_tile(j, hf, carry, False)
        return carry

    acc, _ = lax.fori_loop(0, i, far, carry)
    o_ref[...] = _pair_rows(acc[:, :tq], acc[:, tq:]).T


def _d_attention(qd, kd, vdt):
    t = qd.shape[0]
    tq = tk = min(TQ, t)
    nk = t // tk
    sub = min(256, tk)
    return pl.pallas_call(
        functools.partial(_d_attn_kernel, tq=tq, tk=tk, sub=sub),
        grid=(D_HEADS // 2, t // tq),
        in_specs=[pl.BlockSpec((tq, LANES), lambda g, i: (i, g)),
                  _resident((t, LANES), lambda g, i: (0, g)),
                  _resident((None, nk, LANES, tk), lambda g, i: (g, 0, 0, 0))],
        out_specs=pl.BlockSpec((tq, LANES), lambda g, i: (i, g)),
        out_shape=jax.ShapeDtypeStruct((t, D_HEADS * HEAD_DIM), F32),
        compiler_params=_cparams(("arbitrary", "arbitrary")),
        name="d_attn",
    )(qd, kd, vdt)


def _kmean_kernel(k_ref, o_ref, *, nb):
    k = k_ref[...]
    o_ref[...] = jnp.sum(k.reshape(nb, MOBA_BLOCK, k.shape[1]), axis=1) * (1.0 / MOBA_BLOCK)


def _block_means(c_kv):
    t = c_kv.shape[0]
    nblk = t // MOBA_BLOCK
    nb = 8
    assert nblk % nb == 0
    w = C_HEADS * HEAD_DIM
    return pl.pallas_call(
        functools.partial(_kmean_kernel, nb=nb),
        grid=(nblk // nb,),
        in_specs=[pl.BlockSpec((nb * MOBA_BLOCK, w), lambda i: (i, 0))],
        out_specs=pl.BlockSpec((nb, w), lambda i: (i, 0)),
        out_shape=jax.ShapeDtypeStruct((nblk, w), F32),
        compiler_params=_cparams(("arbitrary",)),
        name="moba_kmean",
    )(c_kv)


def _top_rows(score, k, nrows):
    rows = lax.broadcasted_iota(I32, score.shape, 0)

    def body(_, carry):
        g, sel = carry
        mx = jnp.max(g, axis=0, keepdims=True)
        idx = jnp.min(jnp.where(g == mx, rows, nrows), axis=0, keepdims=True)
        pick = rows == idx
        sel = jnp.where(pick & (mx > 0.5 * NEG_INF), 1.0, sel)
        return jnp.where(pick, M_INIT, g), sel

    _, sel = lax.fori_loop(0, k, body, (score, jnp.zeros(score.shape, F32)))
    return sel


def _c_gate_kernel(q_ref, km_ref, o_ref, *, tq, nblk):
    i = pl.program_id(0)
    lane = lax.broadcasted_iota(I32, (tq, LANES), 1)
    klane = lax.broadcasted_iota(I32, (nblk, LANES), 1)
    rows = lax.broadcasted_iota(I32, (nblk, tq), 0)
    cur = (i * tq + lax.broadcasted_iota(I32, (nblk, tq), 1)) // MOBA_BLOCK
    for h in range(C_HEADS):
        sl = slice((h // 2) * LANES, (h // 2 + 1) * LANES)
        qp = q_ref[:, sl]
        half = (klane >= HEAD_DIM) if h % 2 else (klane < HEAD_DIM)
        km = jnp.where(half, km_ref[:, sl], 0.0)
        a1, a2, a3 = _split3(km)
        b1, b2, b3 = _split3(qp)
        gate = (_dot_nt(a1, b1) + _dot_nt(a1, b2) + _dot_nt(a2, b1)
                + _dot_nt(a2, b2) + _dot_nt(a1, b3) + _dot_nt(a3, b1))
        sel = _top_rows(jnp.where(rows < cur, gate, NEG_INF), MOBA_TOPK, nblk)
        code = jnp.where((sel > 0.0) | (rows == cur), 0.0, NEG_INF)
        parts = [jnp.zeros((HEAD_DIM, tq), F32), code]
        if nblk < HEAD_DIM:
            parts.append(jnp.full((HEAD_DIM - nblk, tq), NEG_INF, F32))
        code_t = jnp.concatenate(parts, axis=0).T
        qlow = qp * SCALE
        if h % 2:
            qlow = pltpu.roll(qlow, HEAD_DIM, 1)
        o_ref[h] = jnp.where(lane < HEAD_DIM, qlow, code_t).astype(BF16)


def _c_gate(cq, kmean):
    t = cq.shape[0]
    tq = min(TQ, t)
    nblk = kmean.shape[0]
    assert nblk <= HEAD_DIM
    return pl.pallas_call(
        functools.partial(_c_gate_kernel, tq=tq, nblk=nblk),
        grid=(t // tq,),
        in_specs=[pl.BlockSpec((tq, cq.shape[1]), lambda i: (i, 0)),
                  pl.BlockSpec(kmean.shape, lambda i: (0, 0))],
        out_specs=pl.BlockSpec((C_HEADS, tq, LANES), lambda i: (0, i, 0)),
        out_shape=jax.ShapeDtypeStruct((C_HEADS, t, LANES), BF16),
        compiler_params=_cparams(("arbitrary",)),
        name="moba_gate",
    )(cq, kmean)


def _c_attn_kernel(q_ref, k_ref, vt_ref, bias_ref, o_ref, *, tq, tk):
    i = pl.program_id(1)
    n = 2 * tq

    def scores(j):
        return jnp.concatenate([_dot_nt(k_ref[e, pl.ds(pl.multiple_of(j * tk, tk), tk), :], q_ref[e])
                                for e in range(2)], axis=1)

    def far(j, carry):
        return _osm_update(scores(j), vt_ref[j], *carry)

    carry = lax.fori_loop(0, jnp.maximum(i - 1, 0), far, _osm_init(n))
    jp = jnp.maximum(i - 1, 0)
    s = scores(jp) + jnp.concatenate([bias_ref[0, 1], bias_ref[1, 1]], axis=1)
    carry = _osm_update(jnp.where(i >= 1, s, NEG_INF), vt_ref[jp], *carry)
    s = scores(i) + jnp.concatenate([bias_ref[0, 0], bias_ref[1, 0]], axis=1)
    m, l, acc = _osm_update(jnp.where(_causal_tile(tk, n, tq), s, NEG_INF), vt_ref[i], *carry)
    o2 = _osm_finish(m, l, acc)
    o_ref[...] = _pair_rows(o2[:, :tq], o2[:, tq:]).T


def _c_attention(cq_aug, ck_aug, cvt, bias):
    t = cq_aug.shape[1]
    tq = tk = min(TQ, t)
    nk = t // tk
    return pl.pallas_call(
        functools.partial(_c_attn_kernel, tq=tq, tk=tk),
        grid=(C_HEADS // 2, t // tq),
        in_specs=[pl.BlockSpec((2, tq, LANES), lambda g, i: (g, i, 0)),
                  _resident((2, t, LANES), lambda g, i: (g, 0, 0)),
                  _resident((None, nk, LANES, tk), lambda g, i: (g, 0, 0, 0)),
                  _resident((2, 2, tk, tq), lambda g, i: (g, 0, 0, 0))],
        out_specs=pl.BlockSpec((tq, LANES), lambda g, i: (i, g)),
        out_shape=jax.ShapeDtypeStruct((t, C_HEADS * HEAD_DIM), F32),
        compiler_params=_cparams(("arbitrary", "arbitrary")),
        name="moba_attn",
    )(cq_aug, ck_aug, cvt, bias)


def _b_comp_kernel(kc_ref, vc_ref, pos_ref, w_ref, first_ref, second_ref, *, nch):
    for c, x_ref in enumerate((kc_ref, vc_ref)):
        for half, o_ref in enumerate((first_ref, second_ref)):
            acc = jnp.zeros((nch, LANES), F32)
            for p in range(CMP_STRIDE):
                x = x_ref[pl.ds(p, nch, stride=CMP_STRIDE), :] + pos_ref[c, half, p]
                acc = acc + _dot(x.astype(BF16), w_ref[c, half, p])
            o_ref[:, c * LANES:(c + 1) * LANES] = acc


def _b_compress(b_kv, pos_bd, w1_bd):
    t = b_kv.shape[0]
    tmc = min(2048, t)
    nch = tmc // CMP_STRIDE
    nc = t // CMP_STRIDE
    return pl.pallas_call(
        functools.partial(_b_comp_kernel, nch=nch),
        grid=(t // tmc,),
        in_specs=[pl.BlockSpec((tmc, LANES), lambda i: (i, 0)),
                  pl.BlockSpec((tmc, LANES), lambda i: (i, 1)),
                  pl.BlockSpec(pos_bd.shape, lambda i: (0, 0, 0, 0, 0)),
                  pl.BlockSpec(w1_bd.shape, lambda i: (0, 0, 0, 0, 0))],
        out_specs=[pl.BlockSpec((nch, 2 * LANES), lambda i: (i, 0)),
                   pl.BlockSpec((nch, 2 * LANES), lambda i: (i, 0))],
        out_shape=[jax.ShapeDtypeStruct((nc, 2 * LANES), F32)] * 2,
        compiler_params=_cparams(("arbitrary",)),
        name="nsa_compress",
    )(b_kv, b_kv, pos_bd, w1_bd)


def _b_cmpfin_kernel(first_ref, second_ref, w2_ref, kc_ref, vct_ref, *, nc):
    h = _silu(first_ref[...] + pltpu.roll(second_ref[...], nc - 1, 0))
    out = _dot(h.astype(BF16), w2_ref[...])
    kc_ref[...] = out[:, :LANES].astype(BF16)
    vct_ref[...] = out[:, LANES:].T.astype(BF16)


def _b_cmp_finish(first, second, w2_bd):
    nb, nc, w = first.shape
    return pl.pallas_call(
        functools.partial(_b_cmpfin_kernel, nc=nc),
        grid=(nb,),
        in_specs=[pl.BlockSpec((None, nc, w), lambda i: (i, 0, 0)),
                  pl.BlockSpec((None, nc, w), lambda i: (i, 0, 0)),
                  pl.BlockSpec(w2_bd.shape, lambda i: (0, 0))],
        out_specs=[pl.BlockSpec((None, nc, LANES), lambda i: (i, 0, 0)),
                   pl.BlockSpec((None, LANES, nc), lambda i: (i, 0, 0))],
        out_shape=[jax.ShapeDtypeStruct((nb, nc, LANES), BF16), jax.ShapeDtypeStruct((nb, LANES, nc), BF16)],
        compiler_params=_cparams(("arbitrary",)),
        name="nsa_cmp_finish",
    )(first, second, w2_bd)


def _kv_rows(acc, kvh):
    return jnp.where(kvh == 0, acc[:HEAD_DIM], acc[HEAD_DIM:])


def _heads_to_lanes(accs, kvh):
    pairs = [jnp.concatenate([_kv_rows(accs[2 * p], kvh), _kv_rows(accs[2 * p + 1], kvh)], axis=0).T
             for p in range(2)]
    return jnp.concatenate(pairs, axis=1)


def _group_queries(qf, kvh, tq):
    lane = lax.broadcasted_iota(I32, (tq, LANES), 1)
    mine = (lane >= HEAD_DIM) == (kvh == 1)
    out = []
    for g in range(B_GROUP):
        slab = qf[:, (g // 2) * LANES:(g // 2 + 1) * LANES]
        moved = jnp.where(kvh == g % 2, slab, pltpu.roll(slab, HEAD_DIM, 1))
        out.append(jnp.where(mine, moved, 0.0).astype(BF16))
    return out


def _b_cmp_kernel(q_ref, kc_ref, vct_ref, pool_ref, cb_ref, oc_ref, code_ref, s_scr, *, tq, nc, pad, ncode):
    kvh = pl.program_id(0)
    i = pl.program_id(1)
    qs = _group_queries(q_ref[...].astype(F32), kvh, tq)
    nrow = lax.broadcasted_iota(I32, (nc, tq), 0)
    tcol = i * tq + lax.broadcasted_iota(I32, (nc, tq), 1)
    valid = tcol - CMP_STRIDE * nrow - (2 * CMP_STRIDE - 1) >= 0
    near = pl.ds(pl.multiple_of(i * (tq // CMP_STRIDE), tq // CMP_STRIDE), 2 * pad)
    imp = jnp.zeros((nc, tq), F32)
    accs = []
    s_scr[0:pad, :] = jnp.full((pad, tq), NEG_INF, F32)
    for g in range(B_GROUP):
        s_scr[pad:pad + nc, :] = jnp.where(valid, _dot_nt(kc_ref[...], qs[g]), NEG_INF)
        s_scr[near, :] = s_scr[near, :] + cb_ref[g]
        s = s_scr[...]
        m = jnp.max(s, axis=0, keepdims=True)
        p = jnp.exp(s - m)
        l = jnp.sum(p, axis=0, keepdims=True)
        pn = (p * jnp.where(m > 0.5 * NEG_INF, 1.0 / jnp.maximum(l, 1e-30), 0.0))[pad:]
        imp = imp + pn
        accs.append(_dot(vct_ref[...], pn.astype(BF16)))
    oc_ref[...] = _heads_to_lanes(accs, kvh)
    i1, i2, i3 = _split3(imp)
    pool = pool_ref[...]
    imp_sel = _dot(pool, i1) + _dot(pool, i2) + _dot(pool, i3)
    nsel = imp_sel.shape[0]
    blk = lax.broadcasted_iota(I32, (nsel, tq), 0)
    cur = (i * tq + lax.broadcasted_iota(I32, (nsel, tq), 1)) // SEL_BLOCK
    forced = (blk == 0) | (blk == cur) | (blk == cur - 1)
    score = jnp.where(blk <= cur, imp_sel + jnp.where(forced, SEL_FORCE, 0.0), NEG_INF)
    sel = _top_rows(score, min(SEL_TOPK, nsel), nsel)
    code = jnp.where(sel > 0.0, 0.0, NEG_INF)
    if nsel < ncode:
        code = jnp.concatenate([code, jnp.full((ncode - nsel, tq), NEG_INF, F32)], axis=0)
    code_ref[...] = code.T.astype(BF16)


def _b_cmp_attention(bq, kcmp, vcmpt, pool, cbias):
    t = bq.shape[0]
    tq = min(TQ, t)
    nc = kcmp.shape[0]
    pad = tq // CMP_STRIDE
    ncode = 4 * HEAD_DIM
    w = B_GROUP * HEAD_DIM
    return pl.pallas_call(
        functools.partial(_b_cmp_kernel, tq=tq, nc=nc, pad=pad, ncode=ncode),
        grid=(B_KV_HEADS, t // tq),
        in_specs=[pl.BlockSpec((tq, w), lambda k, i: (i, k)),
                  pl.BlockSpec(kcmp.shape, lambda k, i: (0, 0)),
                  pl.BlockSpec(vcmpt.shape, lambda k, i: (0, 0)),
                  pl.BlockSpec(pool.shape, lambda k, i: (0, 0)),
                  pl.BlockSpec((B_GROUP, 2 * pad, tq), lambda k, i: (1 + k, 0, 0))],
        out_specs=[pl.BlockSpec((tq, w), lambda k, i: (i, k)),
                   pl.BlockSpec((tq, ncode), lambda k, i: (i, k))],
        out_shape=[jax.ShapeDtypeStruct((t, B_KV_HEADS * w), F32),
                   jax.ShapeDtypeStruct((t, B_KV_HEADS * ncode), BF16)],
        scratch_shapes=[pltpu.VMEM((pad + nc, tq), F32)],
        compiler_params=_cparams(("arbitrary", "arbitrary")),
        name="nsa_cmp_attn",
    )(bq, kcmp, vcmpt, pool, cbias)


def _b_selwin_kernel(q_ref, code_ref, ks_ref, vst_ref, kw_ref, vwt_ref, bias_ref, gc_ref, gs_ref, gw_ref,
                     oc_ref, o_ref, qa_scr, *, tq, tk, nquarter):
    kvh = pl.program_id(0)
    i = pl.program_id(1)
    n = B_GROUP * tq
    qf = q_ref[...].astype(F32)
    cf = code_ref[...].astype(F32)
    lane = lax.broadcasted_iota(I32, (tq, LANES), 1)
    for g in range(B_GROUP):
        slab = qf[:, (g // 2) * LANES:(g // 2 + 1) * LANES]
        qlow = pltpu.roll(slab, HEAD_DIM, 1) if g % 2 else slab
        for qq in range(nquarter):
            cs = cf[:, (qq // 2) * LANES:(qq // 2 + 1) * LANES]
            chigh = cs if qq % 2 else pltpu.roll(cs, HEAD_DIM, 1)
            qa_scr[g * nquarter + qq] = jnp.where(lane < HEAD_DIM, qlow, chigh).astype(BF16)
    per_quarter = HEAD_DIM * SEL_BLOCK // tk

    def scores(j):
        qq = j // per_quarter
        qs = jnp.concatenate([qa_scr[g * nquarter + qq] for g in range(B_GROUP)], axis=0)
        return _dot_nt(_tile_rows(ks_ref, j, tk), qs)

    def far(j, carry):
        return _osm_update(scores(j), vst_ref[j], *carry)

    b1 = jnp.concatenate([bias_ref[g, 1] for g in range(B_GROUP)], axis=1)
    b0 = jnp.concatenate([bias_ref[g, 0] for g in range(B_GROUP)], axis=1)
    causal = _causal_tile(tk, n, tq)
    jp = jnp.maximum(i - 1, 0)
    carry = lax.fori_loop(0, jp, far, _osm_init(n))
    carry = _osm_update(jnp.where(i >= 1, scores(jp) + b1, NEG_INF), vst_ref[jp], *carry)
    carry = _osm_update(jnp.where(causal, scores(i) + b0, NEG_INF), vst_ref[i], *carry)
    o_s = _osm_finish(*carry)

    qw = jnp.concatenate(_group_queries(qf, kvh, tq), axis=0)
    row = lax.broadcasted_iota(I32, (tk, n), 0)
    col = lax.broadcasted_iota(I32, (tk, n), 1) % tq
    s = _dot_nt(_tile_rows(kw_ref, jp, tk), qw) + b1
    carry = _osm_update(jnp.where((row >= col) & (i >= 1), s, NEG_INF), vwt_ref[jp], *_osm_init(n))
    s = _dot_nt(_tile_rows(kw_ref, i, tk), qw) + b0
    carry = _osm_update(jnp.where(causal, s, NEG_INF), vwt_ref[i], *carry)
    o_w = _osm_finish(*carry)

    os_l = _heads_to_lanes([o_s[:, g * tq:(g + 1) * tq] for g in range(B_GROUP)], kvh)
    ow_l = _heads_to_lanes([o_w[:, g * tq:(g + 1) * tq] for g in range(B_GROUP)], kvh)
    o_ref[...] = gc_ref[...] * oc_ref[...] + gs_ref[...] * os_l + gw_ref[...] * ow_l


def _b_selwin_attention(bq, code, ksaug, vst, kw, vwt, bias, gates, o_c):
    t = bq.shape[0]
    tq = tk = min(TQ, t)
    assert tk == WINDOW or t < WINDOW
    nk = t // tk
    w = B_GROUP * HEAD_DIM
    ncode = code.shape[1] // B_KV_HEADS
    nquarter = ncode // HEAD_DIM
    return pl.pallas_call(
        functools.partial(_b_selwin_kernel, tq=tq, tk=tk, nquarter=nquarter),
        grid=(B_KV_HEADS, t // tq),
        in_specs=[pl.BlockSpec((tq, w), lambda k, i: (i, k)),
                  pl.BlockSpec((tq, ncode), lambda k, i: (i, k)),
                  _resident((None, t, LANES), lambda k, i: (k, 0, 0)),
                  _resident((None, nk, LANES, tk), lambda k, i: (0, 0, 0, 0)),
                  _resident((t, LANES), lambda k, i: (0, 0)),
                  _resident((None, nk, LANES, tk), lambda k, i: (0, 0, 0, 0)),
                  _resident((B_GROUP, 2, tk, tq), lambda k, i: (1 + k, 0, 0, 0)),
                  pl.BlockSpec((tq, w), lambda k, i: (i, k)),
                  pl.BlockSpec((tq, w), lambda k, i: (i, 2 + k)),
                  pl.BlockSpec((tq, w), lambda k, i: (i, 4 + k)),
                  pl.BlockSpec((tq, w), lambda k, i: (i, k))],
        out_specs=pl.BlockSpec((tq, w), lambda k, i: (i, k)),
        out_shape=jax.ShapeDtypeStruct((t, B_KV_HEADS * w), F32),
        scratch_shapes=[pltpu.VMEM((B_GROUP * nquarter, tq, LANES), BF16)],
        compiler_params=_cparams(("arbitrary", "arbitrary")),
        name="nsa_selwin_attn",
    )(bq, code, ksaug, vst, kw, vwt, bias, gates, gates, gates, o_c)


_AB = dict(a_q=0, a_k=512, a_v=1024, a_gate=1536, b_q=2048, b_kc=2560, b_ks=2816, b_vs=2944,
           b_kw=3072, b_vw=3200, b_g=3328, b_gate=3352, end=3864)
_ABW = dict(gates=3328, b_gate=3328 + 3 * B_HEADS * HEAD_DIM)
_CD = dict(c_q=0, c_k=512, c_v=1024, c_gate=1536, d_q=2048, d_k=2560, d_v=3072, d_gate=3584)


def _ab_weight(w_in):
    wg = w_in[:, _AB["b_g"]:_AB["b_gate"]].reshape(D_MODEL, B_HEADS, 3)
    wg = jnp.broadcast_to(jnp.transpose(wg, (0, 2, 1))[..., None], (D_MODEL, 3, B_HEADS, HEAD_DIM))
    return jnp.concatenate([w_in[:, :_AB["b_g"]], wg.reshape(D_MODEL, 3 * B_HEADS * HEAD_DIM),
                            w_in[:, _AB["b_gate"]:]], axis=1).astype(BF16)


_AB_OUTS = (
    ("f32", _AB["a_k"], 1024, None),
    ("f32", _AB["b_kc"], 512, None),
    ("f32", _AB["b_kw"], 256, None),
    ("bf16", _AB["a_q"], 512, SCALE),
    ("bf16", _AB["a_k"], 512, 1.0),
    ("vT", _AB["a_v"], 512, None),
    ("silu", _AB["a_gate"], 512, None),
    ("bf16", _AB["b_q"], 512, SCALE),
    ("kaug", _AB["b_ks"], 128, SEL_BLOCK),
    ("vT", _AB["b_vs"], 128, None),
    ("bf16", _AB["b_kw"], 128, 1.0),
    ("vT", _AB["b_vw"], 128, None),
    ("sigmoid", _ABW["gates"], 3 * B_HEADS * HEAD_DIM, None),
    ("silu", _ABW["b_gate"], 512, None),
)

_CD_OUTS = (
    ("f32", _CD["c_k"], 1024, None),
    ("f32", _CD["d_k"], 1024, None),
    ("f32", _CD["c_q"], 512, None),
    ("kaug", _CD["c_k"], 512, MOBA_BLOCK),
    ("vT", _CD["c_v"], 512, None),
    ("silu", _CD["c_gate"], 512, None),
    ("bf16", _CD["d_q"], 512, SCALE),
    ("bf16", _CD["d_k"], 512, 1.0),
    ("vT", _CD["d_v"], 512, None),
    ("silu", _CD["d_gate"], 512, None),
)


def _blockdiag(blocks):
    n = len(blocks)
    rows = []
    for r, b in enumerate(blocks):
        rows.append(jnp.concatenate([b if c == r else jnp.zeros((b.shape[0], blocks[c].shape[1]), b.dtype)
                                     for c in range(n)], axis=1))
    return jnp.concatenate(rows, axis=0)


def _cmp_params(cmp_pos, cmp_w1, cmp_w2):
    pos = cmp_pos.reshape(2, 2, CMP_STRIDE, 1, HEAD_DIM)
    pos_bd = jnp.concatenate([pos, pos], axis=-1)
    w1 = cmp_w1.reshape(2, 2, CMP_STRIDE, HEAD_DIM, HEAD_DIM)
    z = jnp.zeros_like(w1)
    w1_bd = jnp.concatenate([jnp.concatenate([w1, z], axis=-1), jnp.concatenate([z, w1], axis=-1)],
                            axis=-2).astype(BF16)
    w2_bd = _blockdiag([cmp_w2[0], cmp_w2[0], cmp_w2[1], cmp_w2[1]]).astype(BF16)
    return pos_bd, w1_bd, w2_bd


def _pool_matrix(nsel, nc):
    b = np.arange(nsel)[:, None]
    n = np.arange(nc)[None, :]
    per = SEL_BLOCK // CMP_STRIDE
    m = ((n // per) == b).astype(np.float32) + (((n + 1) // per) == b).astype(np.float32)
    return jnp.asarray(m, dtype=BF16)


def _prompt_ab(x, bias, cbias, norm_g, w_ab, w_out, lam_p, head_gain, cmp_prm, li):
    t = x.shape[0]
    (a_kv, b_kv, b_win, qa, ka, vat, sga, bq, ksaug, vst, kw, vwt, gates, sgb) = _project(x, norm_g, w_ab, _AB_OUTS)
    lam_init = 0.8 - 0.6 * math.exp(-0.3 * li)
    o_a = _a_attention(lam_p, qa, ka, vat, bias, head_gain, lam_init)
    pos_bd, w1_bd, w2_bd = cmp_prm
    first, second = _b_compress(b_kv, pos_bd, w1_bd)
    kcmp, vcmpt = _b_cmp_finish(first[None], second[None], w2_bd)
    kcmp, vcmpt = kcmp[0], vcmpt[0]
    nc = t // CMP_STRIDE
    o_c, code = _b_cmp_attention(bq, kcmp, vcmpt, _pool_matrix(nc * CMP_STRIDE // SEL_BLOCK, nc), cbias)
    o_b = _b_selwin_attention(bq, code, ksaug, vst, kw, vwt, bias, gates, o_c)
    y = _out_project(x, o_a, sga, o_b, sgb, w_out, norm_g, False)
    return y, a_kv, b_kv, b_win


def _prompt_cd(x, bias, norm_g, w_cd, w_out, final_g, final):
    (c_kv, d_kv, cq, ck_aug, cvt, sgc, qd, kd, vdt, sgd) = _project(x, norm_g, w_cd, _CD_OUTS)
    cq_aug = _c_gate(cq, _block_means(c_kv))
    o_c = _c_attention(cq_aug, ck_aug, cvt, bias)
    o_d = _d_attention(qd, kd, vdt)
    y = _out_project(x, o_c, sgc, o_d, sgd, w_out, final_g, final)
    return y, c_kv, d_kv


TOK = 8
QROWS = 16
PP = 8


def _query_matrix(q8, vheads, kw, dtype=BF16):
    lane = lax.broadcasted_iota(I32, (TOK, kw), 1)
    rows = []
    for src, roll, lo, hi in vheads:
        slab = q8[:, src:src + kw]
        if roll:
            slab = pltpu.roll(slab, roll, 1)
        rows.append(jnp.where((lane >= lo) & (lane < hi), slab, 0.0))
    pad = LANES // TOK - len(vheads)
    if pad:
        rows.append(jnp.zeros((pad * TOK, kw), F32))
    return jnp.concatenate(rows, axis=0).astype(dtype)


def _s_attn_kernel(pt_ref, q_ref, *refs, vheads, kw, vw, nsteps, pp, tok_major, has_bias, code_blk, code_new,
                   first_mask, stick):
    del pt_ref
    refs = list(refs)
    bias_ref = refs.pop(0) if has_bias else None
    code_ref = refs.pop(0) if code_blk else None
    page_refs = refs[:pp]
    new_ref, o_ref, qm_scr, m_scr, l_scr, acc_scr = refs[pp:]
    s = pl.program_id(1)
    lane = lax.broadcasted_iota(I32, (LANES, PAGE), 1)
    tok = lax.broadcasted_iota(I32, (LANES, PAGE), 0) % TOK

    @pl.when(s == 0)
    def _():
        qm_scr[...] = _query_matrix(q_ref[...].astype(F32)[:TOK], vheads, kw)
        m_scr[...] = jnp.full((LANES, 1), 0.0 if stick else M_INIT, F32)
        l_scr[...] = jnp.zeros((LANES, 1), F32)
        acc_scr[...] = jnp.zeros((LANES, vw), F32)

    qm = qm_scr[...]

    def load(ref):
        if tok_major:
            nk, nv = kw // LANES, vw // LANES
            k = jnp.concatenate([ref[:, h, :] for h in range(nk)], axis=1)
            v = jnp.concatenate([ref[:, nk + h, :] for h in range(nv)], axis=1)
            return k.astype(BF16), v.astype(BF16)
        return ref[0].astype(BF16), ref[1].astype(BF16)

    def qk(k, tm):
        return _dot_nt(qm, k) if tm else _dot(qm, k)

    def pv(p, v, tm):
        return _dot(p, v) if tm else _dot_nt(p, v)

    def load_new():
        page = new_ref[...]
        return page[:, :kw].astype(BF16), page[:, kw:kw + vw].astype(BF16)

    if stick:
        ur = lax.broadcasted_iota(I32, (PAGE, PAGE), 0)
        uc = lax.broadcasted_iota(I32, (PAGE, PAGE), 1)
        later = jnp.where(ur > uc, 1.0, 0.0).astype(BF16)

        def stick_page(k, v, tm, mask):
            z = qk(k, tm)
            ll = -(jnp.maximum(z, 0.0) + jnp.log1p(jnp.exp(-jnp.abs(z))))
            if mask is not None:
                ll = jnp.where(mask, ll, 0.0)
            lh = ll.astype(BF16)
            lo = (ll - lh.astype(F32)).astype(BF16)
            w = jnp.exp(z + ll + _dot(lh, later) + _dot(lo, later) + m_scr[...])
            if mask is not None:
                w = jnp.where(mask, w, 0.0)
            acc_scr[...] = acc_scr[...] + pv(w.astype(BF16), v, tm)
            m_scr[...] = m_scr[...] + jnp.sum(ll, axis=1, keepdims=True)

        @pl.when(s == 0)
        def _():
            k, v = load_new()
            stick_page(k, v, True, lane < tok)

        for r in range(pp):
            k, v = load(page_refs[r])
            stick_page(k, v, tok_major, None)

        @pl.when(s == nsteps - 1)
        def _():
            o_ref[...] = acc_scr[...]
        return

    def update(tiles, values, tm):
        st = tiles[0] if len(tiles) == 1 else jnp.concatenate(tiles, axis=1)
        m_old = m_scr[...]
        m_new = jnp.maximum(m_old, jnp.max(st, axis=1, keepdims=True))
        alpha = jnp.exp(m_old - m_new)
        p = jnp.exp(st - m_new)
        l_scr[...] = alpha * l_scr[...] + jnp.sum(p, axis=1, keepdims=True)
        m_scr[...] = m_new
        acc = acc_scr[...] * alpha
        for r, v in enumerate(values):
            acc = acc + pv(p[:, r * PAGE:(r + 1) * PAGE].astype(BF16), v, tm)
        acc_scr[...] = acc

    def expand_code(first_tok, width):
        nblk = code_ref.shape[1]
        blk = lax.broadcasted_iota(I32, (nblk, width), 0)
        t = first_tok + lax.broadcasted_iota(I32, (nblk, width), 1)
        return _dot(code_ref[...], jnp.where(blk == t // code_blk, 1.0, 0.0).astype(BF16))

    tiles, values = [], []
    for r in range(pp):
        k, v = load(page_refs[r])
        st = qk(k, tok_major)
        if has_bias and r == pp - 1:
            st = st + jnp.where(s == nsteps - 1, bias_ref[0], 0.0)
        if first_mask and r == 0:
            st = jnp.where((lane >= tok) | (s > 0), st, NEG_INF)
        tiles.append(st)
        values.append(v)
    if code_blk:
        cexp = expand_code(s * (pp * PAGE), pp * PAGE)
        tiles = [st + cexp[:, r * PAGE:(r + 1) * PAGE] for r, st in enumerate(tiles)]
    update(tiles, values, tok_major)

    @pl.when(s == nsteps - 1)
    def _():
        k, v = load_new()
        st = qk(k, True)
        if has_bias:
            st = st + bias_ref[1]
        if code_new:
            st = st + expand_code(nsteps * pp * PAGE, PAGE)
        update([jnp.where(lane <= tok, st, NEG_INF)], [v], True)
        m = m_scr[...]
        inv = jnp.where(m > 0.5 * NEG_INF, 1.0 / jnp.maximum(l_scr[...], 1e-30), 0.0)
        o_ref[...] = acc_scr[...] * inv


def _s_attention(page_table, q, pages, page_blk, new, new_col, *, vheads, kw, vw, tok_major=False, bias=None,
                 code=None, code_blk=0, code_new=False, first_mask=False, stick=False, pp=PP, lane_pages=False, name):
    db, npages = page_table.shape
    assert npages % pp == 0
    nsteps = npages // pp
    wq = q.shape[2]

    def page_map(r):
        def idx(b, s, pt):
            g = s * pp + r
            if stick:
                g = npages - 1 - g
            if lane_pages:
                return (b, 0, 0, g)
            return (pt[b, g], 0, 0, 0) if tok_major else (pt[b, g], page_blk, 0, 0)
        return idx

    if tok_major:
        page_block = (None, PAGE, pages.shape[2], LANES)
    else:
        page_block = (None, 2, kw, PAGE)
    in_specs = [pl.BlockSpec((None, QROWS, wq), lambda b, s, pt: (b, 0, 0))]
    args = [q]
    if bias is not None:
        in_specs.append(pl.BlockSpec(bias.shape, lambda b, s, pt: (0, 0, 0)))
        args.append(bias)
    if code is not None:
        in_specs.append(pl.BlockSpec((None,) + code.shape[1:], lambda b, s, pt: (b, 0, 0)))
        args.append(code)
    for r in range(pp):
        in_specs.append(pl.BlockSpec(page_block, page_map(r)))
        args.append(pages)
    in_specs.append(pl.BlockSpec((None, PAGE, kw + vw), lambda b, s, pt: (b, 0, new_col)))
    args.append(new)
    return pl.pallas_call(
        functools.partial(_s_attn_kernel, vheads=tuple(vheads), kw=kw, vw=vw, nsteps=nsteps, pp=pp,
                          tok_major=tok_major, has_bias=bias is not None, code_blk=code_blk, code_new=code_new,
                          first_mask=first_mask, stick=stick),
        grid_spec=pltpu.PrefetchScalarGridSpec(
            num_scalar_prefetch=1, grid=(db, nsteps), in_specs=in_specs,
            out_specs=pl.BlockSpec((None, LANES, vw), lambda b, s, pt: (b, 0, 0)),
            scratch_shapes=[pltpu.VMEM((LANES, kw), BF16), pltpu.VMEM((LANES, 1), F32),
                            pltpu.VMEM((LANES, 1), F32), pltpu.VMEM((LANES, vw), F32)]),
        out_shape=jax.ShapeDtypeStruct((db, LANES, vw), F32),
        compiler_params=_cparams(("arbitrary", "arbitrary")),
        name=name,
    )(page_table, *args)


_S_BIAS_DIST = ((PAGE, 1, False), (0, 1, False), (CMP_STRIDE * TOK - 2 * CMP_STRIDE + 1, CMP_STRIDE, True))


def _s_bias_kernel(tbl_ref, o_ref, *, heads):
    for which, (base, mul, keys_on_rows) in enumerate(_S_BIAS_DIST):
        key = lax.broadcasted_iota(I32, (PAGE, LANES), 0 if keys_on_rows else 1)
        col = lax.broadcasted_iota(I32, (PAGE, LANES), 1 if keys_on_rows else 0)
        dist = base - mul * key + col % TOK
        val = jnp.zeros((PAGE, LANES), F32)
        for h in sorted(set(heads)):
            mine = functools.reduce(lambda a, b: a | b, [col // TOK == v for v, hh in enumerate(heads) if hh == h])
            val = jnp.where(mine, _bias_of_dist(dist, tbl_ref, h), val)
        o_ref[which] = val


def _s_bias(rel_bias, heads):
    return pl.pallas_call(
        functools.partial(_s_bias_kernel, heads=tuple(heads)),
        grid=(1,),
        in_specs=[pl.BlockSpec(memory_space=pltpu.SMEM)],
        out_specs=pl.BlockSpec((3, PAGE, LANES), lambda i: (0, 0, 0)),
        out_shape=jax.ShapeDtypeStruct((3, PAGE, LANES), F32),
        compiler_params=_cparams(("arbitrary",)),
        name="sample_bias",
    )(rel_bias)


def _s_gate_kernel(pt_ref, q_ref, *refs, vheads, nsteps, pp, nblk):
    del pt_ref
    page_refs = refs[:pp]
    code_ref, km_scr = refs[pp:]
    s = pl.program_id(1)
    w = C_HEADS * HEAD_DIM

    @pl.when(s == 0)
    def _():
        km_scr[...] = jnp.zeros((w, LANES), F32)

    per = MOBA_BLOCK // PAGE
    blk_lane = lax.broadcasted_iota(I32, (w, LANES), 1)
    km = km_scr[...]
    for r in range(pp):
        colsum = jnp.sum(page_refs[r][...], axis=1, keepdims=True)
        km = km + jnp.where(blk_lane == (s * pp + r) // per, colsum, 0.0)
    km_scr[...] = km

    @pl.when(s == nsteps - 1)
    def _():
        a1, a2, a3 = _split3(_query_matrix(q_ref[...][:TOK], vheads, w, F32))
        b1, b2, b3 = _split3(km * (1.0 / MOBA_BLOCK))
        gate = (_dot(a1, b1) + _dot(a1, b2) + _dot(a2, b1) + _dot(a2, b2) + _dot(a1, b3) + _dot(a3, b1))
        blk = lax.broadcasted_iota(I32, (LANES, LANES), 0)
        sel = _top_rows(jnp.where(blk < nblk, gate.T, NEG_INF), min(MOBA_TOPK, nblk), LANES)
        code_ref[...] = jnp.where(sel > 0.0, 0.0, NEG_INF).T.astype(BF16)


def _s_moba_gate(page_table, cq, pages, vheads):
    db, npages = page_table.shape
    pp = min(PP, npages)
    nsteps = npages // pp
    nblk = npages * PAGE // MOBA_BLOCK
    assert nblk <= LANES
    w = C_HEADS * HEAD_DIM
    in_specs = [pl.BlockSpec((None, QROWS, w), lambda b, s, pt: (b, 0, 0))]
    for r in range(pp):
        in_specs.append(pl.BlockSpec((None, None, w, PAGE), lambda b, s, pt, r=r: (pt[b, s * pp + r], 0, 0, 0)))
    return pl.pallas_call(
        functools.partial(_s_gate_kernel, vheads=tuple(vheads), nsteps=nsteps, pp=pp, nblk=nblk),
        grid_spec=pltpu.PrefetchScalarGridSpec(
            num_scalar_prefetch=1, grid=(db, nsteps), in_specs=in_specs,
            out_specs=pl.BlockSpec((None, LANES, LANES), lambda b, s, pt: (b, 0, 0)),
            scratch_shapes=[pltpu.VMEM((w, LANES), F32)]),
        out_shape=jax.ShapeDtypeStruct((db, LANES, LANES), BF16),
        compiler_params=_cparams(("arbitrary", "arbitrary")),
        name="sample_moba_gate",
    )(page_table, cq, *([pages] * pp))


CPP = 16


def _s_comp_kernel(pt_ref, *refs, cpp):
    del pt_ref
    page_refs = refs[:cpp]
    pos_ref, w_ref, first_ref, second_ref, x_scr = refs[cpp:]
    per = PAGE // CMP_STRIDE
    for c in range(2):
        for r in range(cpp):
            x_scr[c, r * PAGE:(r + 1) * PAGE, :] = page_refs[r][c].T
        for half, o_ref in enumerate((first_ref, second_ref)):
            acc = jnp.zeros((cpp * per, LANES), F32)
            for p in range(CMP_STRIDE):
                x = x_scr[c, pl.ds(p, cpp * per, stride=CMP_STRIDE), :] + pos_ref[c, half, p]
                acc = acc + _dot(x.astype(BF16), w_ref[c, half, p])
            o_ref[:, c * LANES:(c + 1) * LANES] = acc


def _s_compress(page_table, pages, pos_bd, w1_bd):
    db, npages = page_table.shape
    cpp = min(CPP, npages)
    nsteps = npages // cpp
    per = PAGE // CMP_STRIDE
    nc = npages * per
    in_specs = [pl.BlockSpec((None, 2, LANES, PAGE), lambda b, s, pt, r=r: (pt[b, s * cpp + r], 0, 0, 0))
                for r in range(cpp)]
    in_specs += [pl.BlockSpec(pos_bd.shape, lambda b, s, pt: (0, 0, 0, 0, 0)),
                 pl.BlockSpec(w1_bd.shape, lambda b, s, pt: (0, 0, 0, 0, 0))]
    out_spec = pl.BlockSpec((None, cpp * per, 2 * LANES), lambda b, s, pt: (b, s, 0))
    return pl.pallas_call(
        functools.partial(_s_comp_kernel, cpp=cpp),
        grid_spec=pltpu.PrefetchScalarGridSpec(
            num_scalar_prefetch=1, grid=(db, nsteps), in_specs=in_specs, out_specs=[out_spec, out_spec],
            scratch_shapes=[pltpu.VMEM((2, cpp * PAGE, LANES), F32)]),
        out_shape=[jax.ShapeDtypeStruct((db, nc, 2 * LANES), F32)] * 2,
        compiler_params=_cparams(("arbitrary", "arbitrary")),
        name="sample_nsa_compress",
    )(page_table, *([pages] * cpp), pos_bd, w1_bd)


def _s_cmp_kernel(q_ref, kc_ref, vct_ref, pool_ref, cb_ref, oc_ref, code_ref, s_scr, *, vheads, nc, cur):
    qm = _query_matrix(q_ref[...].astype(F32)[:TOK], vheads, LANES, BF16)
    nrow = lax.broadcasted_iota(I32, (nc, LANES), 0)
    tok = lax.broadcasted_iota(I32, (nc, LANES), 1) % TOK
    valid = CMP_STRIDE * (nc - nrow) + tok - (2 * CMP_STRIDE - 1) >= 0
    s_scr[...] = jnp.where(valid, _dot_nt(kc_ref[...], qm), NEG_INF)
    s_scr[nc - TOK:nc, :] = s_scr[nc - TOK:nc, :] + cb_ref[2, 0:TOK, :]
    s = s_scr[...]
    m = jnp.max(s, axis=0, keepdims=True)
    p = jnp.exp(s - m)
    l = jnp.sum(p, axis=0, keepdims=True)
    pn = p * jnp.where(m > 0.5 * NEG_INF, 1.0 / jnp.maximum(l, 1e-30), 0.0)
    oc_ref[...] = _dot(vct_ref[...], pn.astype(BF16)).T
    gr = lax.broadcasted_iota(I32, (LANES, LANES), 0)
    gc = lax.broadcasted_iota(I32, (LANES, LANES), 1)
    group = jnp.where((gr // (TOK * B_GROUP) == gc // (TOK * B_GROUP)) & (gr % TOK == gc % TOK), 1.0, 0.0).astype(BF16)
    p1, p2, p3 = _split3(pn)
    i1, i2, i3 = _split3(_dot(p1, group) + _dot(p2, group) + _dot(p3, group))
    pool = pool_ref[...]
    imp_sel = _dot(pool, i1) + _dot(pool, i2) + _dot(pool, i3)
    nsel = imp_sel.shape[0]
    blk = lax.broadcasted_iota(I32, (nsel, LANES), 0)
    forced = (blk == 0) | (blk == cur) | (blk == cur - 1)
    score = jnp.where(blk <= cur, imp_sel + jnp.where(forced, SEL_FORCE, 0.0), NEG_INF)
    sel = _top_rows(score, SEL_TOPK, nsel)
    code_ref[...] = jnp.where(sel > 0.0, 0.0, NEG_INF).T.astype(BF16)


def _s_cmp_attention(bq, kcmp, vcmpt, pool, sbias, vheads, cur):
    db = bq.shape[0]
    nc = kcmp.shape[1]
    nsel = pool.shape[0]
    assert nsel % LANES == 0
    return pl.pallas_call(
        functools.partial(_s_cmp_kernel, vheads=tuple(vheads), nc=nc, cur=cur),
        grid=(db,),
        in_specs=[pl.BlockSpec((None, QROWS, bq.shape[2]), lambda b: (b, 0, 0)),
                  pl.BlockSpec((None, nc, LANES), lambda b: (b, 0, 0)),
                  pl.BlockSpec((None, LANES, nc), lambda b: (b, 0, 0)),
                  pl.BlockSpec(pool.shape, lambda b: (0, 0)),
                  pl.BlockSpec(sbias.shape, lambda b: (0, 0, 0))],
        out_specs=[pl.BlockSpec((None, LANES, LANES), lambda b: (b, 0, 0)),
                   pl.BlockSpec((None, LANES, nsel), lambda b: (b, 0, 0))],
        out_shape=[jax.ShapeDtypeStruct((db, LANES, LANES), F32), jax.ShapeDtypeStruct((db, LANES, nsel), BF16)],
        scratch_shapes=[pltpu.VMEM((nc, LANES), F32)],
        compiler_params=_cparams(("arbitrary",)),
        name="sample_nsa_cmp_attn",
    )(bq, kcmp, vcmpt, pool, sbias)


def _s_asm_ab_kernel(lam_ref, gain_ref, oa_ref, oc_ref, os_ref, ow_ref, g_ref, a_out, b_out, *, lam_init):
    lp = lam_ref[...]
    lam = (jnp.exp(jnp.sum(lp[0:1] * lp[1:2], axis=1, keepdims=True))
           - jnp.exp(jnp.sum(lp[2:3] * lp[3:4], axis=1, keepdims=True)) + lam_init)
    for h in range(A_HEADS):
        sl = slice(h * LANES, (h + 1) * LANES)
        o = oa_ref[2 * h * TOK:(2 * h + 1) * TOK, sl] - lam * oa_ref[(2 * h + 1) * TOK:(2 * h + 2) * TOK, sl]
        o = o * lax.rsqrt(jnp.mean(o * o, axis=-1, keepdims=True) + NORM_EPS)
        a_out[:, sl] = o * gain_ref[...] * (1.0 - lam_init)
    lane = lax.broadcasted_iota(I32, (TOK, LANES), 1)
    w = B_HEADS * HEAD_DIM
    for pair in range(B_HEADS // 2):
        acc = jnp.zeros((TOK, LANES), F32)
        for e in range(2):
            h = 2 * pair + e
            kvh = h // B_GROUP
            mine = (lane >= HEAD_DIM) if e else (lane < HEAD_DIM)
            for c, x_ref in enumerate((oc_ref, os_ref, ow_ref)):
                blk = x_ref[h * TOK:(h + 1) * TOK, :]
                if e != kvh:
                    blk = pltpu.roll(blk, HEAD_DIM, 1)
                gate = g_ref[:, c * w + pair * LANES:c * w + (pair + 1) * LANES]
                acc = acc + jnp.where(mine, gate * blk, 0.0)
        b_out[:, pair * LANES:(pair + 1) * LANES] = acc


def _s_assemble_ab(lam_p, gain, oa, oc, os_, ow, gates, lam_init):
    db = oa.shape[0]
    w = B_HEADS * HEAD_DIM
    return pl.pallas_call(
        functools.partial(_s_asm_ab_kernel, lam_init=lam_init),
        grid=(db,),
        in_specs=[pl.BlockSpec(lam_p.shape, lambda b: (0, 0)),
                  pl.BlockSpec((1, LANES), lambda b: (0, 0)),
                  pl.BlockSpec((None, LANES, A_HEADS * LANES), lambda b: (b, 0, 0)),
                  pl.BlockSpec((None, LANES, LANES), lambda b: (b, 0, 0)),
                  pl.BlockSpec((None, LANES, LANES), lambda b: (b, 0, 0)),
                  pl.BlockSpec((None, LANES, LANES), lambda b: (b, 0, 0)),
                  pl.BlockSpec((None, TOK, 3 * w), lambda b: (b, 0, 0))],
        out_specs=[pl.BlockSpec((None, TOK, A_HEADS * LANES), lambda b: (b, 0, 0)),
                   pl.BlockSpec((None, TOK, w), lambda b: (b, 0, 0))],
        out_shape=[jax.ShapeDtypeStruct((db, TOK, A_HEADS * LANES), F32), jax.ShapeDtypeStruct((db, TOK, w), F32)],
        compiler_params=_cparams(("arbitrary",)),
        name="sample_assemble_ab",
    )(lam_p, gain.reshape(1, LANES), oa, oc, os_, ow, gates)


def _s_asm_cd_kernel(oc_ref, od_ref, c_out, d_out):
    w = C_HEADS * HEAD_DIM
    head = lax.broadcasted_iota(I32, (TOK, w), 1) // HEAD_DIM
    for x_ref, o_ref in ((oc_ref, c_out), (od_ref, d_out)):
        out = jnp.zeros((TOK, w), F32)
        for h in range(C_HEADS):
            out = jnp.where(head == h, x_ref[h * TOK:(h + 1) * TOK, :], out)
        o_ref[...] = out


def _s_assemble_cd(oc, od):
    db = oc.shape[0]
    w = C_HEADS * HEAD_DIM
    spec_in = pl.BlockSpec((None, LANES, w), lambda b: (b, 0, 0))
    spec_out = pl.BlockSpec((None, TOK, w), lambda b: (b, 0, 0))
    return pl.pallas_call(
        _s_asm_cd_kernel,
        grid=(db,),
        in_specs=[spec_in, spec_in],
        out_specs=[spec_out, spec_out],
        out_shape=[jax.ShapeDtypeStruct((db, TOK, w), F32)] * 2,
        compiler_params=_cparams(("arbitrary",)),
        name="sample_assemble_cd",
    )(oc, od)


_AB_S_OUTS = (
    ("f32", _AB["a_k"], 1024, None),
    ("f32", _AB["b_kc"], 512, None),
    ("f32", _AB["b_kw"], 256, None),
    ("bf16", _AB["a_q"], 512, SCALE),
    ("silu", _AB["a_gate"], 512, None),
    ("bf16", _AB["b_q"], 512, SCALE),
    ("sigmoid", _ABW["gates"], 3 * B_HEADS * HEAD_DIM, None),
    ("silu", _ABW["b_gate"], 512, None),
)

_CD_S_OUTS = (
    ("f32", _CD["c_k"], 1024, None),
    ("f32", _CD["d_k"], 1024, None),
    ("f32", _CD["c_q"], 512, None),
    ("bf16", _CD["c_q"], 512, SCALE),
    ("silu", _CD["c_gate"], 512, None),
    ("bf16", _CD["d_q"], 512, SCALE),
    ("silu", _CD["d_gate"], 512, None),
)

_VH_A = tuple((0, 0, h * LANES + half * HEAD_DIM, h * LANES + (half + 1) * HEAD_DIM)
              for h in range(A_HEADS) for half in range(2))
_VH_CD = tuple((0, 0, h * HEAD_DIM, (h + 1) * HEAD_DIM) for h in range(C_HEADS))
_VH_B = tuple(((h // 2) * LANES, HEAD_DIM if h % 2 != h // B_GROUP else 0,
               (h // B_GROUP) * HEAD_DIM, (h // B_GROUP + 1) * HEAD_DIM) for h in range(B_HEADS))


def _pad_rows(xs):
    db, dt, d = xs.shape
    return jnp.concatenate([xs, jnp.zeros((db, PAGE - dt, d), xs.dtype)], axis=1).reshape(db * PAGE, d)


def _tok_rows(a, db):
    return a.reshape(db, PAGE, a.shape[-1])[:, :TOK].reshape(db * TOK, a.shape[-1])


def _sample_ab(xs, pt, cache_a, cache_b, cache_win, sb_a, sb_b, norm_g, w_ab, w_out, lam_p, head_gain, cmp_prm, li):
    db, dt = xs.shape[:2]
    npages = pt.shape[1]
    xpad = _pad_rows(xs)
    outs = _project(xpad, norm_g, w_ab, _AB_S_OUTS)
    a_kv, b_kv, b_win, qa, sga, bq, gates, sgb = [a.reshape(db, PAGE, a.shape[-1]) for a in outs]
    lam_init = 0.8 - 0.6 * math.exp(-0.3 * li)
    oa = _s_attention(pt, qa, cache_a, 0, a_kv, 0, vheads=_VH_A, kw=512, vw=512, tok_major=True, bias=sb_a,
                      name="sample_a_attn")
    pos_bd, w1_bd, w2_bd = cmp_prm
    first, second = _s_compress(pt, cache_b, pos_bd, w1_bd)
    kcmp, vcmpt = _b_cmp_finish(first, second, w2_bd)
    nc = npages * (PAGE // CMP_STRIDE)
    cur = npages * PAGE // SEL_BLOCK
    nselp = -(-(cur + 2) // LANES) * LANES
    oc, code = _s_cmp_attention(bq, kcmp, vcmpt, _pool_matrix(nselp, nc), sb_b, _VH_B, cur)
    os_ = _s_attention(pt, bq, cache_b, 1, b_kv, 1, vheads=_VH_B, kw=LANES, vw=LANES, bias=sb_b, code=code,
                       code_blk=SEL_BLOCK, code_new=True, name="sample_nsa_sel_attn")
    nwp = cache_win.shape[3] // PAGE
    ptw = jnp.zeros((db, nwp), I32)
    ow = _s_attention(ptw, bq, cache_win, 0, b_win, 0, vheads=_VH_B, kw=LANES, vw=LANES, bias=sb_b,
                      first_mask=True, pp=nwp, lane_pages=True, name="sample_nsa_win_attn")
    o_a, o_b = _s_assemble_ab(lam_p, head_gain, oa, oc, os_, ow, gates[:, :TOK], lam_init)
    y = _out_project(_tok_rows(xpad, db), o_a.reshape(db * TOK, -1), _tok_rows(sga, db),
                     o_b.reshape(db * TOK, -1), _tok_rows(sgb, db), w_out, norm_g, False)
    return y.reshape(db, TOK, D_MODEL)[:, :dt], a_kv[:, :dt], b_kv[:, :dt], b_win[:, :dt]


def _sample_cd(xs, pt, cache_c, cache_d, sb_c, norm_g, w_cd, w_out, final_g, final):
    db, dt = xs.shape[:2]
    xpad = _pad_rows(xs)
    outs = _project(xpad, norm_g, w_cd, _CD_S_OUTS)
    c_kv, d_kv, cq, cqb, sgc, qd, sgd = [a.reshape(db, PAGE, a.shape[-1]) for a in outs]
    code = _s_moba_gate(pt, cq, cache_c, _VH_CD)
    oc = _s_attention(pt, cqb, cache_c, 0, c_kv, 0, vheads=_VH_CD, kw=512, vw=512, bias=sb_c, code=code,
                      code_blk=MOBA_BLOCK, name="sample_moba_attn")
    od = _s_attention(pt, qd, cache_d, 0, d_kv, 0, vheads=_VH_CD, kw=512, vw=512, stick=True, name="sample_d_attn")
    o_c, o_d = _s_assemble_cd(oc, od)
    y = _out_project(_tok_rows(xpad, db), o_c.reshape(db * TOK, -1), _tok_rows(sgc, db),
                     o_d.reshape(db * TOK, -1), _tok_rows(sgd, db), w_out, final_g, final)
    return y.reshape(db, TOK, D_MODEL)[:, :dt], c_kv[:, :dt], d_kv[:, :dt]


def kernel(x_prompt, x_sample, cache_a_kv, cache_b_kv, cache_b_win, cache_c_kv, cache_d_kv, page_table,
           rel_bias, ab_norm, ab_w_in, ab_w_out, ab_lambda, ab_head_norm, ab_cmp_pos, ab_cmp_w1, ab_cmp_w2,
           cd_norm, cd_w_in, cd_w_out, final_norm):
    t = x_prompt.shape[1]
    db, dt = x_sample.shape[:2]
    assert x_prompt.shape[0] == 1 and dt <= TOK and ab_norm.shape[0] == 1 and cd_norm.shape[0] == 1
    n_phys = cache_a_kv.shape[1]
    tq = min(TQ, t)
    bias = _bias_tiles(rel_bias, tq, tq)
    cbias = _cmp_bias_tiles(rel_bias, 2 * (tq // CMP_STRIDE), tq)
    w_ab = _ab_weight(ab_w_in[0])
    w_cd = cd_w_in[0].astype(BF16)
    cmp_prm = _cmp_params(ab_cmp_pos[0], ab_cmp_w1[0], ab_cmp_w2[0])

    xp, a_kv, b_kv, b_win = _prompt_ab(x_prompt[0], bias, cbias, ab_norm[0], w_ab, ab_w_out[0],
                                       ab_lambda[0], ab_head_norm[0], cmp_prm, 0)
    yp, c_kv, d_kv = _prompt_cd(xp, bias, cd_norm[0], w_cd, cd_w_out[0], final_norm, True)

    sb_a = _s_bias(rel_bias, [h for h in range(A_HEADS) for _ in range(2)])
    sb_b = _s_bias(rel_bias, [A_HEADS + h for h in range(B_HEADS)])
    sb_c = _s_bias(rel_bias, list(range(C_HEADS)))
    cache_a = cache_a_kv[0].reshape(n_phys, PAGE, 2 * A_HEADS, 2 * HEAD_DIM)
    cache_b = jnp.transpose(cache_b_kv[0], (0, 2, 3, 4, 1)).reshape(n_phys, 4, B_KV_HEADS * HEAD_DIM, PAGE)
    cache_c = jnp.transpose(cache_c_kv[0], (0, 2, 3, 4, 1)).reshape(n_phys, 2, C_HEADS * HEAD_DIM, PAGE)
    cache_d = jnp.transpose(cache_d_kv[0], (0, 2, 3, 4, 1)).reshape(n_phys, 2, D_HEADS * HEAD_DIM, PAGE)
    wbuf = cache_b_win.shape[2]
    cache_win = jnp.transpose(cache_b_win[0], (0, 2, 3, 4, 1)).reshape(db, 2, B_KV_HEADS * HEAD_DIM, wbuf)
    xs, a_s, b_s, w_s = _sample_ab(x_sample, page_table, cache_a, cache_b, cache_win, sb_a, sb_b, ab_norm[0], w_ab,
                                   ab_w_out[0], ab_lambda[0], ab_head_norm[0], cmp_prm, 0)
    ys, c_s, d_s = _sample_cd(xs, page_table, cache_c, cache_d, sb_c, cd_norm[0], w_cd, cd_w_out[0], final_norm, True)

    keep = min(WINDOW, t)
    win_s = jnp.concatenate([cache_b_win[0].reshape(db, wbuf, -1), w_s], axis=1)[:, -min(WINDOW, wbuf + dt):]
    return (yp[None], ys,
            a_kv.reshape(1, 1, t, 2, A_HEADS, 2 * HEAD_DIM), a_s.reshape(1, db, dt, 2, A_HEADS, 2 * HEAD_DIM),
            b_kv.reshape(1, 1, t, 4, B_KV_HEADS, HEAD_DIM), b_s.reshape(1, db, dt, 4, B_KV_HEADS, HEAD_DIM),
            b_win[t - keep:].reshape(1, 1, keep, 2, B_KV_HEADS, HEAD_DIM),
            win_s.reshape(1, db, -1, 2, B_KV_HEADS, HEAD_DIM),
            c_kv.reshape(1, 1, t, 2, C_HEADS, HEAD_DIM), c_s.reshape(1, db, dt, 2, C_HEADS, HEAD_DIM),
            d_kv.reshape(1, 1, t, 2, D_HEADS, HEAD_DIM), d_s.reshape(1, db, dt, 2, D_HEADS, HEAD_DIM))
```

```python
import functools
import math

import numpy as np
import jax
import jax.numpy as jnp
from jax import lax
from jax.experimental import pallas as pl
from jax.experimental.pallas import tpu as pltpu

F32 = jnp.float32
BF16 = jnp.bfloat16
I32 = jnp.int32

D_MODEL = 1024
HEAD_DIM = 64
LANES = 128
A_HEADS = 4
B_HEADS = 8
B_KV_HEADS = 2
B_GROUP = 4
C_HEADS = 8
D_HEADS = 8
CMP_STRIDE = 16
SEL_BLOCK = 64
SEL_TOPK = 16
WINDOW = 512
MOBA_BLOCK = 256
MOBA_TOPK = 3
N_BUCKETS = 32
MAX_EXACT = 16
MAX_DISTANCE = 128
PAGE = 128
NORM_EPS = 1e-6
NEG_INF = -1e30
M_INIT = -3.0e38
SEL_FORCE = 1e4
SCALE = HEAD_DIM ** -0.5

TQ = 512
TK = 512
TM = 256
VMEM_LIMIT = 56 * 1024 * 1024


def _bucket_thresholds():
    n = np.arange(0, 4 * MAX_DISTANCE)
    nf = np.maximum(n, 1).astype(np.float64)
    large = MAX_EXACT + (np.log(nf / MAX_EXACT) / math.log(MAX_DISTANCE / MAX_EXACT)
                         * (N_BUCKETS - MAX_EXACT)).astype(np.int32)
    b = np.where(n < MAX_EXACT, n, np.minimum(large, N_BUCKETS - 1))
    return [int(np.argmax(b >= k)) for k in range(N_BUCKETS)]


_THR = _bucket_thresholds()


def _cparams(sem):
    return pltpu.CompilerParams(dimension_semantics=sem, vmem_limit_bytes=VMEM_LIMIT)


def _silu(x):
    return x * jax.nn.sigmoid(x)


def _dot_nt(a, b):
    return lax.dot_general(a, b, (((1,), (1,)), ((), ())), preferred_element_type=F32)


def _dot(a, b):
    return jnp.dot(a, b, preferred_element_type=F32)


def _split3(x):
    x1 = x.astype(BF16)
    r = x - x1.astype(F32)
    x2 = r.astype(BF16)
    x3 = (r - x2.astype(F32)).astype(BF16)
    return x1, x2, x3


def _bias_of_dist(dist, tbl_ref, h):
    val = jnp.full(dist.shape, tbl_ref[0, h], F32)
    for b in range(1, N_BUCKETS):
        val = jnp.where(dist >= _THR[b], tbl_ref[b, h], val)
    return jnp.where(dist >= 0, val - tbl_ref[N_BUCKETS - 1, h], 0.0)


def _bias_kernel(tbl_ref, o_ref, *, tk, tq):
    h = pl.program_id(0)
    d = pl.program_id(1)
    row = lax.broadcasted_iota(I32, (tk, tq), 0)
    col = lax.broadcasted_iota(I32, (tk, tq), 1)
    o_ref[...] = _bias_of_dist(d * tk + col - row, tbl_ref, h)


def _bias_tiles(rel_bias, tk, tq):
    nh = rel_bias.shape[1]
    return pl.pallas_call(
        functools.partial(_bias_kernel, tk=tk, tq=tq),
        grid=(nh, 2),
        in_specs=[pl.BlockSpec(memory_space=pltpu.SMEM)],
        out_specs=pl.BlockSpec((None, None, tk, tq), lambda h, d: (h, d, 0, 0)),
        out_shape=jax.ShapeDtypeStruct((nh, 2, tk, tq), F32),
        compiler_params=_cparams(("arbitrary", "arbitrary")),
        name="bias_tiles",
    )(rel_bias)


def _cmp_bias_kernel(tbl_ref, o_ref, *, nr, tq):
    h = pl.program_id(0)
    row = lax.broadcasted_iota(I32, (nr, tq), 0)
    col = lax.broadcasted_iota(I32, (nr, tq), 1)
    o_ref[...] = _bias_of_dist(col - CMP_STRIDE * row + (tq - 2 * CMP_STRIDE + 1), tbl_ref, h)


def _cmp_bias_tiles(rel_bias, nr, tq):
    nh = rel_bias.shape[1]
    return pl.pallas_call(
        functools.partial(_cmp_bias_kernel, nr=nr, tq=tq),
        grid=(nh,),
        in_specs=[pl.BlockSpec(memory_space=pltpu.SMEM)],
        out_specs=pl.BlockSpec((None, nr, tq), lambda h: (h, 0, 0)),
        out_shape=jax.ShapeDtypeStruct((nh, nr, tq), F32),
        compiler_params=_cparams(("arbitrary",)),
        name="cmp_bias_tiles",
    )(rel_bias)


def _proj_kernel(x_ref, g_ref, w_ref, *o_refs, outs, tm):
    i = pl.program_id(0)
    x = x_ref[...]
    h = x * lax.rsqrt(jnp.mean(x * x, axis=-1, keepdims=True) + NORM_EPS) * g_ref[...]
    hb = h.astype(BF16)
    lane = lax.broadcasted_iota(I32, (tm, LANES), 1)
    rowg = i * tm + lax.broadcasted_iota(I32, (tm, LANES), 0)
    for o_ref, (kind, c0, width, arg) in zip(o_refs, outs):
        r = _dot(hb, w_ref[:, c0:c0 + width])
        if kind == "f32":
            o_ref[...] = r
        elif kind == "silu":
            o_ref[...] = _silu(r)
        elif kind == "sigmoid":
            o_ref[...] = jax.nn.sigmoid(r)
        elif kind == "bf16":
            o_ref[...] = (r * arg).astype(BF16)
        elif kind == "vT":
            for b in range(width // LANES):
                o_ref[b] = r[:, b * LANES:(b + 1) * LANES].T.astype(BF16)
        elif kind == "kaug":
            onehot = jnp.where(lane - HEAD_DIM == (rowg // arg) % HEAD_DIM, 1.0, 0.0)
            for hh in range(width // HEAD_DIM):
                slab = r[:, (hh // 2) * LANES:(hh // 2 + 1) * LANES]
                if hh % 2:
                    slab = pltpu.roll(slab, HEAD_DIM, 1)
                o_ref[hh] = jnp.where(lane < HEAD_DIM, slab, onehot).astype(BF16)
        else:
            raise ValueError(kind)


def _project(x2d, gain, w, outs):
    t = x2d.shape[0]
    tm = min(TM, t)
    vt_w = min(TK, t)
    assert t % tm == 0 and vt_w % tm == 0
    out_shapes, out_specs = [], []
    for kind, c0, width, arg in outs:
        if kind in ("f32", "silu", "sigmoid"):
            out_shapes.append(jax.ShapeDtypeStruct((t, width), F32))
            out_specs.append(pl.BlockSpec((tm, width), lambda i: (i, 0)))
        elif kind == "bf16":
            out_shapes.append(jax.ShapeDtypeStruct((t, width), BF16))
            out_specs.append(pl.BlockSpec((tm, width), lambda i: (i, 0)))
        elif kind == "vT":
            nb = width // LANES
            per = vt_w // tm
            out_shapes.append(jax.ShapeDtypeStruct((nb, t // vt_w, LANES, vt_w), BF16))
            out_specs.append(pl.BlockSpec((nb, None, LANES, tm), lambda i, per=per: (0, i // per, 0, i % per)))
        elif kind == "kaug":
            nh = width // HEAD_DIM
            out_shapes.append(jax.ShapeDtypeStruct((nh, t, LANES), BF16))
            out_specs.append(pl.BlockSpec((nh, tm, LANES), lambda i: (0, i, 0)))
    return pl.pallas_call(
        functools.partial(_proj_kernel, outs=tuple(outs), tm=tm),
        grid=(t // tm,),
        in_specs=[pl.BlockSpec((tm, D_MODEL), lambda i: (i, 0)),
                  pl.BlockSpec((1, D_MODEL), lambda i: (0, 0)),
                  pl.BlockSpec(w.shape, lambda i: (0, 0))],
        out_specs=out_specs,
        out_shape=out_shapes,
        compiler_params=_cparams(("arbitrary",)),
        name="in_proj",
    )(x2d, gain.reshape(1, D_MODEL), w)


def _outproj_kernel(x_ref, o1_ref, g1_ref, o2_ref, g2_ref, w_ref, fg_ref, y_ref, *, final):
    mixed = jnp.concatenate([o1_ref[...] * g1_ref[...], o2_ref[...] * g2_ref[...]], axis=1)
    y = x_ref[...] + _dot(mixed.astype(BF16), w_ref[...])
    if final:
        y = y * lax.rsqrt(jnp.mean(y * y, axis=-1, keepdims=True) + NORM_EPS) * fg_ref[...]
    y_ref[...] = y


def _out_project(x2d, o1, g1, o2, g2, w_out, final_gain, final):
    t = x2d.shape[0]
    tm = min(TM, t)
    half = D_MODEL // 2
    return pl.pallas_call(
        functools.partial(_outproj_kernel, final=final),
        grid=(t // tm,),
        in_specs=[pl.BlockSpec((tm, D_MODEL), lambda i: (i, 0)),
                  pl.BlockSpec((tm, half), lambda i: (i, 0)),
                  pl.BlockSpec((tm, half), lambda i: (i, 0)),
                  pl.BlockSpec((tm, half), lambda i: (i, 0)),
                  pl.BlockSpec((tm, half), lambda i: (i, 0)),
                  pl.BlockSpec((D_MODEL, D_MODEL), lambda i: (0, 0)),
                  pl.BlockSpec((1, D_MODEL), lambda i: (0, 0))],
        out_specs=pl.BlockSpec((tm, D_MODEL), lambda i: (i, 0)),
        out_shape=jax.ShapeDtypeStruct((t, D_MODEL), F32),
        compiler_params=_cparams(("arbitrary",)),
        name="out_proj",
    )(x2d, o1, g1, o2, g2, w_out.astype(BF16), final_gain.reshape(1, D_MODEL))


def _osm_update(s, vt, m, l, acc):
    m_new = jnp.maximum(m, jnp.max(s, axis=0, keepdims=True))
    alpha = jnp.exp(m - m_new)
    p = jnp.exp(s - m_new)
    l = alpha * l + jnp.sum(p, axis=0, keepdims=True)
    acc = alpha * acc + _dot(vt, p.astype(BF16))
    return m_new, l, acc


def _osm_init(n):
    return (jnp.full((1, n), M_INIT, F32), jnp.zeros((1, n), F32), jnp.zeros((LANES, n), F32))


def _osm_finish(m, l, acc):
    inv = jnp.where(m > 0.5 * NEG_INF, 1.0 / jnp.maximum(l, 1e-30), 0.0)
    return acc * inv


def _tile_rows(ref, j, tk):
    return ref[pl.ds(pl.multiple_of(j * tk, tk), tk), :]


def _causal_sweep(scores, vt_ref, i, n, bias1, bias0, causal):
    nfar = jnp.maximum(i - 1, 0)

    def far(j, carry):
        return _osm_update(scores(j), vt_ref[j], *carry)

    carry = lax.fori_loop(0, nfar, far, _osm_init(n))
    carry = _osm_update(jnp.where(i >= 1, scores(nfar) + bias1, NEG_INF), vt_ref[nfar], *carry)
    return _osm_update(jnp.where(causal, scores(i) + bias0, NEG_INF), vt_ref[i], *carry)


def _causal_tile(tk, n, tq):
    row = lax.broadcasted_iota(I32, (tk, n), 0)
    col = lax.broadcasted_iota(I32, (tk, n), 1) % tq
    return row <= col


def _pair_rows(acc_even, acc_odd):
    row = lax.broadcasted_iota(I32, acc_even.shape, 0)
    return jnp.where(row < HEAD_DIM, acc_even, acc_odd)


def _a_attn_kernel(lam_ref, q_ref, k_ref, vt_ref, bias_ref, gain_ref, o_ref, *, tq, tk, lam_init):
    i = pl.program_id(1)
    q = q_ref[...]
    lane = lax.broadcasted_iota(I32, (tq, LANES), 1)
    zero = jnp.zeros_like(q)
    qs = jnp.concatenate([jnp.where(lane < HEAD_DIM, q, zero), jnp.where(lane >= HEAD_DIM, q, zero)], axis=0)
    n = 2 * tq

    def scores(j):
        return _dot_nt(_tile_rows(k_ref, j, tk), qs)

    b1 = bias_ref[1]
    b0 = bias_ref[0]
    m, l, acc = _causal_sweep(scores, vt_ref, i, n, jnp.concatenate([b1, b1], axis=1),
                              jnp.concatenate([b0, b0], axis=1), _causal_tile(tk, n, tq))
    o2 = _osm_finish(m, l, acc)
    lp = lam_ref[...]
    lam = (jnp.exp(jnp.sum(lp[0:1] * lp[1:2], axis=1, keepdims=True))
           - jnp.exp(jnp.sum(lp[2:3] * lp[3:4], axis=1, keepdims=True)) + lam_init)
    o = (o2[:, :tq] - lam * o2[:, tq:]).T
    o = o * lax.rsqrt(jnp.mean(o * o, axis=-1, keepdims=True) + NORM_EPS)
    o_ref[...] = o * gain_ref[...] * (1.0 - lam_init)


def _a_attention(lam_p, qa, ka, vat, bias, gain, lam_init):
    t = qa.shape[0]
    tq = tk = min(TQ, t)
    nk = t // tk
    return pl.pallas_call(
        functools.partial(_a_attn_kernel, tq=tq, tk=tk, lam_init=lam_init),
        grid=(A_HEADS, t // tq),
        in_specs=[pl.BlockSpec(lam_p.shape, lambda h, i: (0, 0)),
                  pl.BlockSpec((tq, LANES), lambda h, i: (i, h)),
                  pl.BlockSpec((t, LANES), lambda h, i: (0, h)),
                  pl.BlockSpec((None, nk, LANES, tk), lambda h, i: (h, 0, 0, 0)),
                  pl.BlockSpec((None, 2, tk, tq), lambda h, i: (h, 0, 0, 0)),
                  pl.BlockSpec((1, LANES), lambda h, i: (0, 0))],
        out_specs=pl.BlockSpec((tq, LANES), lambda h, i: (i, h)),
        out_shape=jax.ShapeDtypeStruct((t, A_HEADS * LANES), F32),
        compiler_params=_cparams(("arbitrary", "arbitrary")),
        name="a_attn",
    )(lam_p, qa, ka, vat, bias, gain.reshape(1, LANES))


def _resident(shape, index_map):
    return pl.BlockSpec(shape, index_map, pipeline_mode=pl.Buffered(1))


def _d_attn_kernel(q_ref, k_ref, vt_ref, o_ref, *, tq, tk, sub):
    i = pl.program_id(1)
    q = q_ref[...]
    lane = lax.broadcasted_iota(I32, (tq, LANES), 1)
    zero = jnp.zeros_like(q)
    qs = jnp.concatenate([jnp.where(lane < HEAD_DIM, q, zero), jnp.where(lane >= HEAD_DIM, q, zero)], axis=0)
    n = 2 * tq
    ur = lax.broadcasted_iota(I32, (sub, sub), 0)
    uc = lax.broadcasted_iota(I32, (sub, sub), 1)
    upper = jnp.where(uc > ur, 1.0, 0.0).astype(BF16)
    row = lax.broadcasted_iota(I32, (sub, n), 0)
    col = lax.broadcasted_iota(I32, (sub, n), 1) % tq
    nsub = tk // sub

    def sub_tile(j, hf, carry, diag):
        acc, run = carry
        z = _dot_nt(k_ref[pl.ds(pl.multiple_of(j * tk + hf * sub, sub), sub), :], qs)
        sp = jnp.maximum(z, 0.0) + jnp.log(1.0 + jnp.exp(-jnp.abs(z)))
        if diag:
            mask = (row + hf * sub) < col
            sp = jnp.where(mask, sp, 0.0)
        w = jnp.exp(z - sp - _dot(upper, sp.astype(BF16)) - run)
        if diag:
            w = jnp.where(mask, w, 0.0)
        acc = acc + _dot(vt_ref[j, :, hf * sub:(hf + 1) * sub], w.astype(BF16))
        return acc, run + jnp.sum(sp, axis=0, keepdims=True)

    carry = (jnp.zeros((LANES, n), F32), jnp.zeros((1, n), F32))
    for hf in reversed(range(nsub)):
        carry = sub_tile(i, hf, carry, True)

    def far(jj, carry):
        j = i - 1 - jj
        for hf in reversed(range(nsub)):
            carry = sub_tile(j, hf, carry, False)
        return carry

    acc, _ = lax.fori_loop(0, i, far, carry)
    o_ref[...] = _pair_rows(acc[:, :tq], acc[:, tq:]).T


def _d_attention(qd, kd, vdt):
    t = qd.shape[0]
    tq = tk = min(TQ, t)
    nk = t // tk
    sub = min(256, tk)
    return pl.pallas_call(
        functools.partial(_d_attn_kernel, tq=tq, tk=tk, sub=sub),
        grid=(D_HEADS // 2, t // tq),
        in_specs=[pl.BlockSpec((tq, LANES), lambda g, i: (i, g)),
                  _resident((t, LANES), lambda g, i: (0, g)),
                  _resident((None, nk, LANES, tk), lambda g, i: (g, 0, 0, 0))],
        out_specs=pl.BlockSpec((tq, LANES), lambda g, i: (i, g)),
        out_shape=jax.ShapeDtypeStruct((t, D_HEADS * HEAD_DIM), F32),
        compiler_params=_cparams(("arbitrary", "arbitrary")),
        name="d_attn",
    )(qd, kd, vdt)


def _kmean_kernel(k_ref, o_ref, *, nb):
    k = k_ref[...]
    o_ref[...] = jnp.sum(k.reshape(nb, MOBA_BLOCK, k.shape[1]), axis=1) * (1.0 / MOBA_BLOCK)


def _block_means(c_kv):
    t = c_kv.shape[0]
    nblk = t // MOBA_BLOCK
    nb = 8
    assert nblk % nb == 0
    w = C_HEADS * HEAD_DIM
    return pl.pallas_call(
        functools.partial(_kmean_kernel, nb=nb),
        grid=(nblk // nb,),
        in_specs=[pl.BlockSpec((nb * MOBA_BLOCK, w), lambda i: (i, 0))],
        out_specs=pl.BlockSpec((nb, w), lambda i: (i, 0)),
        out_shape=jax.ShapeDtypeStruct((nblk, w), F32),
        compiler_params=_cparams(("arbitrary",)),
        name="moba_kmean",
    )(c_kv)


def _top_rows(score, k, nrows):
    rows = lax.broadcasted_iota(I32, score.shape, 0)

    def body(_, carry):
        g, sel = carry
        mx = jnp.max(g, axis=0, keepdims=True)
        idx = jnp.min(jnp.where(g == mx, rows, nrows), axis=0, keepdims=True)
        pick = rows == idx
        sel = jnp.where(pick & (mx > 0.5 * NEG_INF), 1.0, sel)
        return jnp.where(pick, M_INIT, g), sel

    _, sel = lax.fori_loop(0, k, body, (score, jnp.zeros(score.shape, F32)))
    return sel


def _c_gate_kernel(q_ref, km_ref, o_ref, *, tq, nblk):
    i = pl.program_id(0)
    lane = lax.broadcasted_iota(I32, (tq, LANES), 1)
    klane = lax.broadcasted_iota(I32, (nblk, LANES), 1)
    rows = lax.broadcasted_iota(I32, (nblk, tq), 0)
    cur = (i * tq + lax.broadcasted_iota(I32, (nblk, tq), 1)) // MOBA_BLOCK
    for h in range(C_HEADS):
        sl = slice((h // 2) * LANES, (h // 2 + 1) * LANES)
        qp = q_ref[:, sl]
        half = (klane >= HEAD_DIM) if h % 2 else (klane < HEAD_DIM)
        km = jnp.where(half, km_ref[:, sl], 0.0)
        a1, a2, a3 = _split3(km)
        b1, b2, b3 = _split3(qp)
        gate = (_dot_nt(a1, b1) + _dot_nt(a1, b2) + _dot_nt(a2, b1)
                + _dot_nt(a2, b2) + _dot_nt(a1, b3) + _dot_nt(a3, b1))
        sel = _top_rows(jnp.where(rows < cur, gate, NEG_INF), MOBA_TOPK, nblk)
        code = jnp.where((sel > 0.0) | (rows == cur), 0.0, NEG_INF)
        parts = [jnp.zeros((HEAD_DIM, tq), F32), code]
        if nblk < HEAD_DIM:
            parts.append(jnp.full((HEAD_DIM - nblk, tq), NEG_INF, F32))
        code_t = jnp.concatenate(parts, axis=0).T
        qlow = qp * SCALE
        if h % 2:
            qlow = pltpu.roll(qlow, HEAD_DIM, 1)
        o_ref[h] = jnp.where(lane < HEAD_DIM, qlow, code_t).astype(BF16)


def _c_gate(cq, kmean):
    t = cq.shape[0]
    tq = min(TQ, t)
    nblk = kmean.shape[0]
    assert nblk <= HEAD_DIM
    return pl.pallas_call(
        functools.partial(_c_gate_kernel, tq=tq, nblk=nblk),
        grid=(t // tq,),
        in_specs=[pl.BlockSpec((tq, cq.shape[1]), lambda i: (i, 0)),
                  pl.BlockSpec(kmean.shape, lambda i: (0, 0))],
        out_specs=pl.BlockSpec((C_HEADS, tq, LANES), lambda i: (0, i, 0)),
        out_shape=jax.ShapeDtypeStruct((C_HEADS, t, LANES), BF16),
        compiler_params=_cparams(("arbitrary",)),
        name="moba_gate",
    )(cq, kmean)


def _c_attn_kernel(q_ref, k_ref, vt_ref, bias_ref, o_ref, *, tq, tk):
    i = pl.program_id(1)
    n = 2 * tq

    def scores(j):
        return jnp.concatenate([_dot_nt(k_ref[e, pl.ds(pl.multiple_of(j * tk, tk), tk), :], q_ref[e])
                                for e in range(2)], axis=1)

    m, l, acc = _causal_sweep(scores, vt_ref, i, n,
                              jnp.concatenate([bias_ref[0, 1], bias_ref[1, 1]], axis=1),
                              jnp.concatenate([bias_ref[0, 0], bias_ref[1, 0]], axis=1), _causal_tile(tk, n, tq))
    o2 = _osm_finish(m, l, acc)
    o_ref[...] = _pair_rows(o2[:, :tq], o2[:, tq:]).T


def _c_attention(cq_aug, ck_aug, cvt, bias):
    t = cq_aug.shape[1]
    tq = tk = min(TQ, t)
    nk = t // tk
    return pl.pallas_call(
        functools.partial(_c_attn_kernel, tq=tq, tk=tk),
        grid=(C_HEADS // 2, t // tq),
        in_specs=[pl.BlockSpec((2, tq, LANES), lambda g, i: (g, i, 0)),
                  _resident((2, t, LANES), lambda g, i: (g, 0, 0)),
                  _resident((None, nk, LANES, tk), lambda g, i: (g, 0, 0, 0)),
                  _resident((2, 2, tk, tq), lambda g, i: (g, 0, 0, 0))],
        out_specs=pl.BlockSpec((tq, LANES), lambda g, i: (i, g)),
        out_shape=jax.ShapeDtypeStruct((t, C_HEADS * HEAD_DIM), F32),
        compiler_params=_cparams(("arbitrary", "arbitrary")),
        name="moba_attn",
    )(cq_aug, ck_aug, cvt, bias)


def _b_comp_kernel(kc_ref, vc_ref, pos_ref, w_ref, first_ref, second_ref, *, nch):
    for c, x_ref in enumerate((kc_ref, vc_ref)):
        for half, o_ref in enumerate((first_ref, second_ref)):
            acc = jnp.zeros((nch, LANES), F32)
            for p in range(CMP_STRIDE):
                x = x_ref[pl.ds(p, nch, stride=CMP_STRIDE), :] + pos_ref[c, half, p]
                acc = acc + _dot(x.astype(BF16), w_ref[c, half, p])
            o_ref[:, c * LANES:(c + 1) * LANES] = acc


def _b_compress(b_kv, pos_bd, w1_bd):
    t = b_kv.shape[0]
    tmc = min(2048, t)
    nch = tmc // CMP_STRIDE
    nc = t // CMP_STRIDE
    return pl.pallas_call(
        functools.partial(_b_comp_kernel, nch=nch),
        grid=(t // tmc,),
        in_specs=[pl.BlockSpec((tmc, LANES), lambda i: (i, 0)),
                  pl.BlockSpec((tmc, LANES), lambda i: (i, 1)),
                  pl.BlockSpec(pos_bd.shape, lambda i: (0, 0, 0, 0, 0)),
                  pl.BlockSpec(w1_bd.shape, lambda i: (0, 0, 0, 0, 0))],
        out_specs=[pl.BlockSpec((nch, 2 * LANES), lambda i: (i, 0)),
                   pl.BlockSpec((nch, 2 * LANES), lambda i: (i, 0))],
        out_shape=[jax.ShapeDtypeStruct((nc, 2 * LANES), F32)] * 2,
        compiler_params=_cparams(("arbitrary",)),
        name="nsa_compress",
    )(b_kv, b_kv, pos_bd, w1_bd)


def _b_cmpfin_kernel(first_ref, second_ref, w2_ref, kc_ref, vct_ref, *, nc):
    h = _silu(first_ref[...] + pltpu.roll(second_ref[...], nc - 1, 0))
    out = _dot(h.astype(BF16), w2_ref[...])
    kc_ref[...] = out[:, :LANES].astype(BF16)
    vct_ref[...] = out[:, LANES:].T.astype(BF16)


def _b_cmp_finish(first, second, w2_bd):
    nb, nc, w = first.shape
    return pl.pallas_call(
        functools.partial(_b_cmpfin_kernel, nc=nc),
        grid=(nb,),
        in_specs=[pl.BlockSpec((None, nc, w), lambda i: (i, 0, 0)),
                  pl.BlockSpec((None, nc, w), lambda i: (i, 0, 0)),
                  pl.BlockSpec(w2_bd.shape, lambda i: (0, 0))],
        out_specs=[pl.BlockSpec((None, nc, LANES), lambda i: (i, 0, 0)),
                   pl.BlockSpec((None, LANES, nc), lambda i: (i, 0, 0))],
        out_shape=[jax.ShapeDtypeStruct((nb, nc, LANES), BF16), jax.ShapeDtypeStruct((nb, LANES, nc), BF16)],
        compiler_params=_cparams(("arbitrary",)),
        name="nsa_cmp_finish",
    )(first, second, w2_bd)


def _kv_rows(acc, kvh):
    return jnp.where(kvh == 0, acc[:HEAD_DIM], acc[HEAD_DIM:])


def _heads_to_lanes(accs, kvh):
    pairs = [jnp.concatenate([_kv_rows(accs[2 * p], kvh), _kv_rows(accs[2 * p + 1], kvh)], axis=0).T
             for p in range(2)]
    return jnp.concatenate(pairs, axis=1)


def _group_queries(qf, kvh, tq):
    lane = lax.broadcasted_iota(I32, (tq, LANES), 1)
    mine = (lane >= HEAD_DIM) == (kvh == 1)
    out = []
    for g in range(B_GROUP):
        slab = qf[:, (g // 2) * LANES:(g // 2 + 1) * LANES]
        moved = jnp.where(kvh == g % 2, slab, pltpu.roll(slab, HEAD_DIM, 1))
        out.append(jnp.where(mine, moved, 0.0).astype(BF16))
    return out


def _b_cmp_kernel(q_ref, kc_ref, vct_ref, pool_ref, cb_ref, oc_ref, code_ref, s_scr, *, tq, nc, pad, ncode):
    kvh = pl.program_id(0)
    i = pl.program_id(1)
    qs = _group_queries(q_ref[...].astype(F32), kvh, tq)
    nrow = lax.broadcasted_iota(I32, (nc, tq), 0)
    tcol = i * tq + lax.broadcasted_iota(I32, (nc, tq), 1)
    valid = tcol - CMP_STRIDE * nrow - (2 * CMP_STRIDE - 1) >= 0
    near = pl.ds(pl.multiple_of(i * (tq // CMP_STRIDE), tq // CMP_STRIDE), 2 * pad)
    imp = jnp.zeros((nc, tq), F32)
    accs = []
    s_scr[0:pad, :] = jnp.full((pad, tq), NEG_INF, F32)
    for g in range(B_GROUP):
        s_scr[pad:pad + nc, :] = jnp.where(valid, _dot_nt(kc_ref[...], qs[g]), NEG_INF)
        s_scr[near, :] = s_scr[near, :] + cb_ref[g]
        s = s_scr[...]
        m = jnp.max(s, axis=0, keepdims=True)
        p = jnp.exp(s - m)
        l = jnp.sum(p, axis=0, keepdims=True)
        pn = (p * jnp.where(m > 0.5 * NEG_INF, 1.0 / jnp.maximum(l, 1e-30), 0.0))[pad:]
        imp = imp + pn
        accs.append(_dot(vct_ref[...], pn.astype(BF16)))
    oc_ref[...] = _heads_to_lanes(accs, kvh)
    i1, i2, i3 = _split3(imp)
    pool = pool_ref[...]
    imp_sel = _dot(pool, i1) + _dot(pool, i2) + _dot(pool, i3)
    nsel = imp_sel.shape[0]
    blk = lax.broadcasted_iota(I32, (nsel, tq), 0)
    cur = (i * tq + lax.broadcasted_iota(I32, (nsel, tq), 1)) // SEL_BLOCK
    forced = (blk == 0) | (blk == cur) | (blk == cur - 1)
    score = jnp.where(blk <= cur, imp_sel + jnp.where(forced, SEL_FORCE, 0.0), NEG_INF)
    sel = _top_rows(score, min(SEL_TOPK, nsel), nsel)
    code = jnp.where(sel > 0.0, 0.0, NEG_INF)
    if nsel < ncode:
        code = jnp.concatenate([code, jnp.full((ncode - nsel, tq), NEG_INF, F32)], axis=0)
    code_ref[...] = code.T.astype(BF16)


def _b_cmp_attention(bq, kcmp, vcmpt, pool, cbias):
    t = bq.shape[0]
    tq = min(TQ, t)
    nc = kcmp.shape[0]
    pad = tq // CMP_STRIDE
    ncode = 4 * HEAD_DIM
    w = B_GROUP * HEAD_DIM
    return pl.pallas_call(
        functools.partial(_b_cmp_kernel, tq=tq, nc=nc, pad=pad, ncode=ncode),
        grid=(B_KV_HEADS, t // tq),
        in_specs=[pl.BlockSpec((tq, w), lambda k, i: (i, k)),
                  pl.BlockSpec(kcmp.shape, lambda k, i: (0, 0)),
                  pl.BlockSpec(vcmpt.shape, lambda k, i: (0, 0)),
                  pl.BlockSpec(pool.shape, lambda k, i: (0, 0)),
                  pl.BlockSpec((B_GROUP, 2 * pad, tq), lambda k, i: (1 + k, 0, 0))],
        out_specs=[pl.BlockSpec((tq, w), lambda k, i: (i, k)),
                   pl.BlockSpec((tq, ncode), lambda k, i: (i, k))],
        out_shape=[jax.ShapeDtypeStruct((t, B_KV_HEADS * w), F32),
                   jax.ShapeDtypeStruct((t, B_KV_HEADS * ncode), BF16)],
        scratch_shapes=[pltpu.VMEM((pad + nc, tq), F32)],
        compiler_params=_cparams(("arbitrary", "arbitrary")),
        name="nsa_cmp_attn",
    )(bq, kcmp, vcmpt, pool, cbias)


def _b_selwin_kernel(q_ref, code_ref, ks_ref, vst_ref, kw_ref, vwt_ref, bias_ref, gc_ref, gs_ref, gw_ref,
                     oc_ref, o_ref, qa_scr, *, tq, tk, nquarter):
    kvh = pl.program_id(0)
    i = pl.program_id(1)
    n = B_GROUP * tq
    qf = q_ref[...].astype(F32)
    cf = code_ref[...].astype(F32)
    lane = lax.broadcasted_iota(I32, (tq, LANES), 1)
    for g in range(B_GROUP):
        slab = qf[:, (g // 2) * LANES:(g // 2 + 1) * LANES]
        qlow = pltpu.roll(slab, HEAD_DIM, 1) if g % 2 else slab
        for qq in range(nquarter):
            cs = cf[:, (qq // 2) * LANES:(qq // 2 + 1) * LANES]
            chigh = cs if qq % 2 else pltpu.roll(cs, HEAD_DIM, 1)
            qa_scr[g * nquarter + qq] = jnp.where(lane < HEAD_DIM, qlow, chigh).astype(BF16)
    per_quarter = HEAD_DIM * SEL_BLOCK // tk

    def scores(j):
        qq = j // per_quarter
        qs = jnp.concatenate([qa_scr[g * nquarter + qq] for g in range(B_GROUP)], axis=0)
        return _dot_nt(_tile_rows(ks_ref, j, tk), qs)

    b1 = jnp.concatenate([bias_ref[g, 1] for g in range(B_GROUP)], axis=1)
    b0 = jnp.concatenate([bias_ref[g, 0] for g in range(B_GROUP)], axis=1)
    causal = _causal_tile(tk, n, tq)
    jp = jnp.maximum(i - 1, 0)
    o_s = _osm_finish(*_causal_sweep(scores, vst_ref, i, n, b1, b0, causal))

    qw = jnp.concatenate(_group_queries(qf, kvh, tq), axis=0)
    row = lax.broadcasted_iota(I32, (tk, n), 0)
    col = lax.broadcasted_iota(I32, (tk, n), 1) % tq
    s = _dot_nt(_tile_rows(kw_ref, jp, tk), qw) + b1
    carry = _osm_update(jnp.where((row >= col) & (i >= 1), s, NEG_INF), vwt_ref[jp], *_osm_init(n))
    s = _dot_nt(_tile_rows(kw_ref, i, tk), qw) + b0
    carry = _osm_update(jnp.where(causal, s, NEG_INF), vwt_ref[i], *carry)
    o_w = _osm_finish(*carry)

    os_l = _heads_to_lanes([o_s[:, g * tq:(g + 1) * tq] for g in range(B_GROUP)], kvh)
    ow_l = _heads_to_lanes([o_w[:, g * tq:(g + 1) * tq] for g in range(B_GROUP)], kvh)
    o_ref[...] = gc_ref[...] * oc_ref[...] + gs_ref[...] * os_l + gw_ref[...] * ow_l


def _b_selwin_attention(bq, code, ksaug, vst, kw, vwt, bias, gates, o_c):
    t = bq.shape[0]
    tq = tk = min(TQ, t)
    assert tk == WINDOW or t < WINDOW
    nk = t // tk
    w = B_GROUP * HEAD_DIM
    ncode = code.shape[1] // B_KV_HEADS
    nquarter = ncode // HEAD_DIM
    return pl.pallas_call(
        functools.partial(_b_selwin_kernel, tq=tq, tk=tk, nquarter=nquarter),
        grid=(B_KV_HEADS, t // tq),
        in_specs=[pl.BlockSpec((tq, w), lambda k, i: (i, k)),
                  pl.BlockSpec((tq, ncode), lambda k, i: (i, k)),
                  _resident((None, t, LANES), lambda k, i: (k, 0, 0)),
                  _resident((None, nk, LANES, tk), lambda k, i: (0, 0, 0, 0)),
                  _resident((t, LANES), lambda k, i: (0, 0)),
                  _resident((None, nk, LANES, tk), lambda k, i: (0, 0, 0, 0)),
                  _resident((B_GROUP, 2, tk, tq), lambda k, i: (1 + k, 0, 0, 0)),
                  pl.BlockSpec((tq, w), lambda k, i: (i, k)),
                  pl.BlockSpec((tq, w), lambda k, i: (i, 2 + k)),
                  pl.BlockSpec((tq, w), lambda k, i: (i, 4 + k)),
                  pl.BlockSpec((tq, w), lambda k, i: (i, k))],
        out_specs=pl.BlockSpec((tq, w), lambda k, i: (i, k)),
        out_shape=jax.ShapeDtypeStruct((t, B_KV_HEADS * w), F32),
        scratch_shapes=[pltpu.VMEM((B_GROUP * nquarter, tq, LANES), BF16)],
        compiler_params=_cparams(("arbitrary", "arbitrary")),
        name="nsa_selwin_attn",
    )(bq, code, ksaug, vst, kw, vwt, bias, gates, gates, gates, o_c)


_AB = dict(a_q=0, a_k=512, a_v=1024, a_gate=1536, b_q=2048, b_kc=2560, b_ks=2816, b_vs=2944,
           b_kw=3072, b_vw=3200, b_g=3328, b_gate=3352, end=3864)
_ABW = dict(gates=3328, b_gate=3328 + 3 * B_HEADS * HEAD_DIM)
_CD = dict(c_q=0, c_k=512, c_v=1024, c_gate=1536, d_q=2048, d_k=2560, d_v=3072, d_gate=3584)


def _ab_weight(w_in):
    wg = w_in[:, _AB["b_g"]:_AB["b_gate"]].reshape(D_MODEL, B_HEADS, 3)
    wg = jnp.broadcast_to(jnp.transpose(wg, (0, 2, 1))[..., None], (D_MODEL, 3, B_HEADS, HEAD_DIM))
    return jnp.concatenate([w_in[:, :_AB["b_g"]], wg.reshape(D_MODEL, 3 * B_HEADS * HEAD_DIM),
                            w_in[:, _AB["b_gate"]:]], axis=1).astype(BF16)


_AB_OUTS = (
    ("f32", _AB["a_k"], 1024, None),
    ("f32", _AB["b_kc"], 512, None),
    ("f32", _AB["b_kw"], 256, None),
    ("bf16", _AB["a_q"], 512, SCALE),
    ("bf16", _AB["a_k"], 512, 1.0),
    ("vT", _AB["a_v"], 512, None),
    ("silu", _AB["a_gate"], 512, None),
    ("bf16", _AB["b_q"], 512, SCALE),
    ("kaug", _AB["b_ks"], 128, SEL_BLOCK),
    ("vT", _AB["b_vs"], 128, None),
    ("bf16", _AB["b_kw"], 128, 1.0),
    ("vT", _AB["b_vw"], 128, None),
    ("sigmoid", _ABW["gates"], 3 * B_HEADS * HEAD_DIM, None),
    ("silu", _ABW["b_gate"], 512, None),
)

_CD_OUTS = (
    ("f32", _CD["c_k"], 1024, None),
    ("f32", _CD["d_k"], 1024, None),
    ("f32", _CD["c_q"], 512, None),
    ("kaug", _CD["c_k"], 512, MOBA_BLOCK),
    ("vT", _CD["c_v"], 512, None),
    ("silu", _CD["c_gate"], 512, None),
    ("bf16", _CD["d_q"], 512, SCALE),
    ("bf16", _CD["d_k"], 512, 1.0),
    ("vT", _CD["d_v"], 512, None),
    ("silu", _CD["d_gate"], 512, None),
)


def _blockdiag(blocks):
    n = len(blocks)
    rows = []
    for r, b in enumerate(blocks):
        rows.append(jnp.concatenate([b if c == r else jnp.zeros((b.shape[0], blocks[c].shape[1]), b.dtype)
                                     for c in range(n)], axis=1))
    return jnp.concatenate(rows, axis=0)


def _cmp_params(cmp_pos, cmp_w1, cmp_w2):
    pos = cmp_pos.reshape(2, 2, CMP_STRIDE, 1, HEAD_DIM)
    pos_bd = jnp.concatenate([pos, pos], axis=-1)
    w1 = cmp_w1.reshape(2, 2, CMP_STRIDE, HEAD_DIM, HEAD_DIM)
    z = jnp.zeros_like(w1)
    w1_bd = jnp.concatenate([jnp.concatenate([w1, z], axis=-1), jnp.concatenate([z, w1], axis=-1)],
                            axis=-2).astype(BF16)
    w2_bd = _blockdiag([cmp_w2[0], cmp_w2[0], cmp_w2[1], cmp_w2[1]]).astype(BF16)
    return pos_bd, w1_bd, w2_bd


def _pool_matrix(nsel, nc):
    b = np.arange(nsel)[:, None]
    n = np.arange(nc)[None, :]
    per = SEL_BLOCK // CMP_STRIDE
    m = ((n // per) == b).astype(np.float32) + (((n + 1) // per) == b).astype(np.float32)
    return jnp.asarray(m, dtype=BF16)


def _prompt_ab(x, bias, cbias, norm_g, w_ab, w_out, lam_p, head_gain, cmp_prm, li):
    t = x.shape[0]
    (a_kv, b_kv, b_win, qa, ka, vat, sga, bq, ksaug, vst, kw, vwt, gates, sgb) = _project(x, norm_g, w_ab, _AB_OUTS)
    lam_init = 0.8 - 0.6 * math.exp(-0.3 * li)
    o_a = _a_attention(lam_p, qa, ka, vat, bias, head_gain, lam_init)
    pos_bd, w1_bd, w2_bd = cmp_prm
    first, second = _b_compress(b_kv, pos_bd, w1_bd)
    kcmp, vcmpt = _b_cmp_finish(first[None], second[None], w2_bd)
    kcmp, vcmpt = kcmp[0], vcmpt[0]
    nc = t // CMP_STRIDE
    o_c, code = _b_cmp_attention(bq, kcmp, vcmpt, _pool_matrix(nc * CMP_STRIDE // SEL_BLOCK, nc), cbias)
    o_b = _b_selwin_attention(bq, code, ksaug, vst, kw, vwt, bias, gates, o_c)
    y = _out_project(x, o_a, sga, o_b, sgb, w_out, norm_g, False)
    return y, a_kv, b_kv, b_win


def _prompt_cd(x, bias, norm_g, w_cd, w_out, final_g, final):
    (c_kv, d_kv, cq, ck_aug, cvt, sgc, qd, kd, vdt, sgd) = _project(x, norm_g, w_cd, _CD_OUTS)
    cq_aug = _c_gate(cq, _block_means(c_kv))
    o_c = _c_attention(cq_aug, ck_aug, cvt, bias)
    o_d = _d_attention(qd, kd, vdt)
    y = _out_project(x, o_c, sgc, o_d, sgd, w_out, final_g, final)
    return y, c_kv, d_kv


TOK = 8
QROWS = 16
PP = 8


def _query_matrix(q8, vheads, kw, dtype=BF16):
    lane = lax.broadcasted_iota(I32, (TOK, kw), 1)
    rows = []
    for src, roll, lo, hi in vheads:
        slab = q8[:, src:src + kw]
        if roll:
            slab = pltpu.roll(slab, roll, 1)
        rows.append(jnp.where((lane >= lo) & (lane < hi), slab, 0.0))
    pad = LANES // TOK - len(vheads)
    if pad:
        rows.append(jnp.zeros((pad * TOK, kw), F32))
    return jnp.concatenate(rows, axis=0).astype(dtype)


def _s_attn_kernel(pt_ref, q_ref, *refs, vheads, kw, vw, nsteps, pp, tok_major, has_bias, code_blk, code_new,
                   first_mask, stick):
    del pt_ref
    refs = list(refs)
    bias_ref = refs.pop(0) if has_bias else None
    code_ref = refs.pop(0) if code_blk else None
    page_refs = refs[:pp]
    new_ref, o_ref, qm_scr, m_scr, l_scr, acc_scr = refs[pp:]
    s = pl.program_id(1)
    lane = lax.broadcasted_iota(I32, (LANES, PAGE), 1)
    tok = lax.broadcasted_iota(I32, (LANES, PAGE), 0) % TOK

    @pl.when(s == 0)
    def _():
        qm_scr[...] = _query_matrix(q_ref[...].astype(F32)[:TOK], vheads, kw)
        m_scr[...] = jnp.full((LANES, 1), 0.0 if stick else M_INIT, F32)
        l_scr[...] = jnp.zeros((LANES, 1), F32)
        acc_scr[...] = jnp.zeros((LANES, vw), F32)

    qm = qm_scr[...]

    def load(ref):
        if tok_major:
            nk, nv = kw // LANES, vw // LANES
            k = jnp.concatenate([ref[pl.ds(h, PAGE, stride=nk + nv), :] for h in range(nk)], axis=1)
            v = jnp.concatenate([ref[pl.ds(nk + h, PAGE, stride=nk + nv), :] for h in range(nv)], axis=1)
            return k.astype(BF16), v.astype(BF16)
        return ref[0].astype(BF16), ref[1].astype(BF16)

    def qk(k, tm):
        return _dot_nt(qm, k) if tm else _dot(qm, k)

    def pv(p, v, tm):
        return _dot(p, v) if tm else _dot_nt(p, v)

    def load_new():
        page = new_ref[...]
        return page[:, :kw].astype(BF16), page[:, kw:kw + vw].astype(BF16)

    if stick:
        ur = lax.broadcasted_iota(I32, (PAGE, PAGE), 0)
        uc = lax.broadcasted_iota(I32, (PAGE, PAGE), 1)
        later = jnp.where(ur > uc, 1.0, 0.0).astype(BF16)

        def stick_pages(kvs, tm, mask):
            zs = [qk(k, tm) for k, _ in kvs]
            sps = [jnp.maximum(z, 0.0) + jnp.log(1.0 + jnp.exp(-jnp.abs(z))) for z in zs]
            if mask is not None:
                sps = [jnp.where(mask, sp, 0.0) for sp in sps]
            afters = [_dot(sp.astype(BF16), later) for sp in sps]
            run = m_scr[...]
            acc = acc_scr[...]
            for z, sp, after, (_, v) in zip(zs, sps, afters, kvs):
                w = jnp.exp(z - sp - after - run)
                if mask is not None:
                    w = jnp.where(mask, w, 0.0)
                acc = acc + pv(w.astype(BF16), v, tm)
                run = run + jnp.sum(sp, axis=1, keepdims=True)
            acc_scr[...] = acc
            m_scr[...] = run

        @pl.when(s == 0)
        def _():
            stick_pages([load_new()], True, lane < tok)

        stick_pages([load(page_refs[r]) for r in range(pp)], tok_major, None)

        @pl.when(s == nsteps - 1)
        def _():
            o_ref[...] = acc_scr[...]
        return

    def update(tiles, values, tm):
        st = tiles[0] if len(tiles) == 1 else jnp.concatenate(tiles, axis=1)
        m_old = m_scr[...]
        m_new = jnp.maximum(m_old, jnp.max(st, axis=1, keepdims=True))
        alpha = jnp.exp(m_old - m_new)
        p = jnp.exp(st - m_new)
        l_scr[...] = alpha * l_scr[...] + jnp.sum(p, axis=1, keepdims=True)
        m_scr[...] = m_new
        acc = acc_scr[...] * alpha
        for r, v in enumerate(values):
            acc = acc + pv(p[:, r * PAGE:(r + 1) * PAGE].astype(BF16), v, tm)
        acc_scr[...] = acc

    def expand_code(first_tok, width):
        nblk = code_ref.shape[1]
        blk = lax.broadcasted_iota(I32, (nblk, width), 0)
        t = first_tok + lax.broadcasted_iota(I32, (nblk, width), 1)
        return _dot(code_ref[...], jnp.where(blk == t // code_blk, 1.0, 0.0).astype(BF16))

    tiles, values = [], []
    for r in range(pp):
        k, v = load(page_refs[r])
        st = qk(k, tok_major)
        if has_bias and r == pp - 1:
            st = st + jnp.where(s == nsteps - 1, bias_ref[0], 0.0)
        if first_mask and r == 0:
            st = jnp.where((lane >= tok) | (s > 0), st, NEG_INF)
        tiles.append(st)
        values.append(v)
    if code_blk:
        cexp = expand_code(s * (pp * PAGE), pp * PAGE)
        tiles = [st + cexp[:, r * PAGE:(r + 1) * PAGE] for r, st in enumerate(tiles)]
    update(tiles, values, tok_major)

    @pl.when(s == nsteps - 1)
    def _():
        k, v = load_new()
        st = qk(k, True)
        if has_bias:
            st = st + bias_ref[1]
        if code_new:
            st = st + expand_code(nsteps * pp * PAGE, PAGE)
        update([jnp.where(lane <= tok, st, NEG_INF)], [v], True)
        m = m_scr[...]
        inv = jnp.where(m > 0.5 * NEG_INF, 1.0 / jnp.maximum(l_scr[...], 1e-30), 0.0)
        o_ref[...] = acc_scr[...] * inv


def _s_attention(page_table, q, pages, page_blk, new, new_col, *, vheads, kw, vw, tok_major=False, bias=None,
                 code=None, code_blk=0, code_new=False, first_mask=False, stick=False, pp=PP, lane_pages=False, name):
    db, npages = page_table.shape
    assert npages % pp == 0
    nsteps = npages // pp
    wq = q.shape[2]

    def page_map(r):
        def idx(b, s, pt):
            g = s * pp + r
            if stick:
                g = npages - 1 - g
            if lane_pages:
                return (b, 0, 0, g)
            return (pt[b, g], 0, 0) if tok_major else (pt[b, g], page_blk, 0, 0)
        return idx

    if tok_major:
        assert pages.shape[1:] == (PAGE * (kw + vw) // LANES, LANES)
        page_block = (None,) + pages.shape[1:]
    else:
        page_block = (None, 2, kw, PAGE)
    in_specs = [pl.BlockSpec((None, QROWS, wq), lambda b, s, pt: (b, 0, 0))]
    args = [q]
    if bias is not None:
        in_specs.append(pl.BlockSpec(bias.shape, lambda b, s, pt: (0, 0, 0)))
        args.append(bias)
    if code is not None:
        in_specs.append(pl.BlockSpec((None,) + code.shape[1:], lambda b, s, pt: (b, 0, 0)))
        args.append(code)
    for r in range(pp):
        in_specs.append(pl.BlockSpec(page_block, page_map(r)))
        args.append(pages)
    in_specs.append(pl.BlockSpec((None, PAGE, kw + vw), lambda b, s, pt: (b, 0, new_col)))
    args.append(new)
    return pl.pallas_call(
        functools.partial(_s_attn_kernel, vheads=tuple(vheads), kw=kw, vw=vw, nsteps=nsteps, pp=pp,
                          tok_major=tok_major, has_bias=bias is not None, code_blk=code_blk, code_new=code_new,
                          first_mask=first_mask, stick=stick),
        grid_spec=pltpu.PrefetchScalarGridSpec(
            num_scalar_prefetch=1, grid=(db, nsteps), in_specs=in_specs,
            out_specs=pl.BlockSpec((None, LANES, vw), lambda b, s, pt: (b, 0, 0)),
            scratch_shapes=[pltpu.VMEM((LANES, kw), BF16), pltpu.VMEM((LANES, 1), F32),
                            pltpu.VMEM((LANES, 1), F32), pltpu.VMEM((LANES, vw), F32)]),
        out_shape=jax.ShapeDtypeStruct((db, LANES, vw), F32),
        compiler_params=_cparams(("arbitrary", "arbitrary")),
        name=name,
    )(page_table, *args)


_S_BIAS_DIST = ((PAGE, 1, False), (0, 1, False), (CMP_STRIDE * TOK - 2 * CMP_STRIDE + 1, CMP_STRIDE, True))


def _s_bias_kernel(tbl_ref, o_ref, *, heads):
    for which, (base, mul, keys_on_rows) in enumerate(_S_BIAS_DIST):
        key = lax.broadcasted_iota(I32, (PAGE, LANES), 0 if keys_on_rows else 1)
        col = lax.broadcasted_iota(I32, (PAGE, LANES), 1 if keys_on_rows else 0)
        dist = base - mul * key + col % TOK
        val = jnp.zeros((PAGE, LANES), F32)
        for h in sorted(set(heads)):
            mine = functools.reduce(lambda a, b: a | b, [col // TOK == v for v, hh in enumerate(heads) if hh == h])
            val = jnp.where(mine, _bias_of_dist(dist, tbl_ref, h), val)
        o_ref[which] = val


def _s_bias(rel_bias, heads):
    return pl.pallas_call(
        functools.partial(_s_bias_kernel, heads=tuple(heads)),
        grid=(1,),
        in_specs=[pl.BlockSpec(memory_space=pltpu.SMEM)],
        out_specs=pl.BlockSpec((3, PAGE, LANES), lambda i: (0, 0, 0)),
        out_shape=jax.ShapeDtypeStruct((3, PAGE, LANES), F32),
        compiler_params=_cparams(("arbitrary",)),
        name="sample_bias",
    )(rel_bias)


def _s_gate_kernel(pt_ref, q_ref, *refs, vheads, nsteps, pp, nblk):
    del pt_ref
    page_refs = refs[:pp]
    code_ref, km_scr = refs[pp:]
    s = pl.program_id(1)
    w = C_HEADS * HEAD_DIM

    @pl.when(s == 0)
    def _():
        km_scr[...] = jnp.zeros((w, LANES), F32)

    per = MOBA_BLOCK // PAGE
    blk_lane = lax.broadcasted_iota(I32, (w, LANES), 1)
    km = km_scr[...]
    for r in range(pp):
        colsum = jnp.sum(page_refs[r][...], axis=1, keepdims=True)
        km = km + jnp.where(blk_lane == (s * pp + r) // per, colsum, 0.0)
    km_scr[...] = km

    @pl.when(s == nsteps - 1)
    def _():
        a1, a2, a3 = _split3(_query_matrix(q_ref[...][:TOK], vheads, w, F32))
        b1, b2, b3 = _split3(km * (1.0 / MOBA_BLOCK))
        gate = (_dot(a1, b1) + _dot(a1, b2) + _dot(a2, b1) + _dot(a2, b2) + _dot(a1, b3) + _dot(a3, b1))
        blk = lax.broadcasted_iota(I32, (LANES, LANES), 0)
        sel = _top_rows(jnp.where(blk < nblk, gate.T, NEG_INF), min(MOBA_TOPK, nblk), LANES)
        code_ref[...] = jnp.where(sel > 0.0, 0.0, NEG_INF).T.astype(BF16)


def _s_moba_gate(page_table, cq, pages, vheads):
    db, npages = page_table.shape
    pp = min(PP, npages)
    nsteps = npages // pp
    nblk = npages * PAGE // MOBA_BLOCK
    assert nblk <= LANES
    w = C_HEADS * HEAD_DIM
    in_specs = [pl.BlockSpec((None, QROWS, w), lambda b, s, pt: (b, 0, 0))]
    for r in range(pp):
        in_specs.append(pl.BlockSpec((None, None, w, PAGE), lambda b, s, pt, r=r: (pt[b, s * pp + r], 0, 0, 0)))
    return pl.pallas_call(
        functools.partial(_s_gate_kernel, vheads=tuple(vheads), nsteps=nsteps, pp=pp, nblk=nblk),
        grid_spec=pltpu.PrefetchScalarGridSpec(
            num_scalar_prefetch=1, grid=(db, nsteps), in_specs=in_specs,
            out_specs=pl.BlockSpec((None, LANES, LANES), lambda b, s, pt: (b, 0, 0)),
            scratch_shapes=[pltpu.VMEM((w, LANES), F32)]),
        out_shape=jax.ShapeDtypeStruct((db, LANES, LANES), BF16),
        compiler_params=_cparams(("arbitrary", "arbitrary")),
        name="sample_moba_gate",
    )(page_table, cq, *([pages] * pp))


CPP = 16


def _s_comp_kernel(pt_ref, *refs, cpp):
    del pt_ref
    page_refs = refs[:cpp]
    pos_ref, w_ref, first_ref, second_ref, x_scr = refs[cpp:]
    per = PAGE // CMP_STRIDE
    for c in range(2):
        for r in range(cpp):
            x_scr[c, r * PAGE:(r + 1) * PAGE, :] = page_refs[r][c].T
        for half, o_ref in enumerate((first_ref, second_ref)):
            acc = jnp.zeros((cpp * per, LANES), F32)
            for p in range(CMP_STRIDE):
                x = x_scr[c, pl.ds(p, cpp * per, stride=CMP_STRIDE), :] + pos_ref[c, half, p]
                acc = acc + _dot(x.astype(BF16), w_ref[c, half, p])
            o_ref[:, c * LANES:(c + 1) * LANES] = acc


def _s_compress(page_table, pages, pos_bd, w1_bd):
    db, npages = page_table.shape
    cpp = min(CPP, npages)
    nsteps = npages // cpp
    per = PAGE // CMP_STRIDE
    nc = npages * per
    in_specs = [pl.BlockSpec((None, 2, LANES, PAGE), lambda b, s, pt, r=r: (pt[b, s * cpp + r], 0, 0, 0))
                for r in range(cpp)]
    in_specs += [pl.BlockSpec(pos_bd.shape, lambda b, s, pt: (0, 0, 0, 0, 0)),
                 pl.BlockSpec(w1_bd.shape, lambda b, s, pt: (0, 0, 0, 0, 0))]
    out_spec = pl.BlockSpec((None, cpp * per, 2 * LANES), lambda b, s, pt: (b, s, 0))
    return pl.pallas_call(
        functools.partial(_s_comp_kernel, cpp=cpp),
        grid_spec=pltpu.PrefetchScalarGridSpec(
            num_scalar_prefetch=1, grid=(db, nsteps), in_specs=in_specs, out_specs=[out_spec, out_spec],
            scratch_shapes=[pltpu.VMEM((2, cpp * PAGE, LANES), F32)]),
        out_shape=[jax.ShapeDtypeStruct((db, nc, 2 * LANES), F32)] * 2,
        compiler_params=_cparams(("arbitrary", "arbitrary")),
        name="sample_nsa_compress",
    )(page_table, *([pages] * cpp), pos_bd, w1_bd)


def _s_cmp_kernel(q_ref, kc_ref, vct_ref, pool_ref, cb_ref, oc_ref, code_ref, s_scr, *, vheads, nc, cur):
    qm = _query_matrix(q_ref[...].astype(F32)[:TOK], vheads, LANES, BF16)
    nrow = lax.broadcasted_iota(I32, (nc, LANES), 0)
    tok = lax.broadcasted_iota(I32, (nc, LANES), 1) % TOK
    valid = CMP_STRIDE * (nc - nrow) + tok - (2 * CMP_STRIDE - 1) >= 0
    s_scr[...] = jnp.where(valid, _dot_nt(kc_ref[...], qm), NEG_INF)
    s_scr[nc - TOK:nc, :] = s_scr[nc - TOK:nc, :] + cb_ref[2, 0:TOK, :]
    s = s_scr[...]
    m = jnp.max(s, axis=0, keepdims=True)
    p = jnp.exp(s - m)
    l = jnp.sum(p, axis=0, keepdims=True)
    pn = p * jnp.where(m > 0.5 * NEG_INF, 1.0 / jnp.maximum(l, 1e-30), 0.0)
    oc_ref[...] = _dot(vct_ref[...], pn.astype(BF16)).T
    gr = lax.broadcasted_iota(I32, (LANES, LANES), 0)
    gc = lax.broadcasted_iota(I32, (LANES, LANES), 1)
    group = jnp.where((gr // (TOK * B_GROUP) == gc // (TOK * B_GROUP)) & (gr % TOK == gc % TOK), 1.0, 0.0).astype(BF16)
    p1, p2, p3 = _split3(pn)
    i1, i2, i3 = _split3(_dot(p1, group) + _dot(p2, group) + _dot(p3, group))
    pool = pool_ref[...]
    imp_sel = _dot(pool, i1) + _dot(pool, i2) + _dot(pool, i3)
    nsel = imp_sel.shape[0]
    blk = lax.broadcasted_iota(I32, (nsel, LANES), 0)
    forced = (blk == 0) | (blk == cur) | (blk == cur - 1)
    score = jnp.where(blk <= cur, imp_sel + jnp.where(forced, SEL_FORCE, 0.0), NEG_INF)
    sel = _top_rows(score, SEL_TOPK, nsel)
    code_ref[...] = jnp.where(sel > 0.0, 0.0, NEG_INF).T.astype(BF16)


def _s_cmp_attention(bq, kcmp, vcmpt, pool, sbias, vheads, cur):
    db = bq.shape[0]
    nc = kcmp.shape[1]
    nsel = pool.shape[0]
    assert nsel % LANES == 0
    return pl.pallas_call(
        functools.partial(_s_cmp_kernel, vheads=tuple(vheads), nc=nc, cur=cur),
        grid=(db,),
        in_specs=[pl.BlockSpec((None, QROWS, bq.shape[2]), lambda b: (b, 0, 0)),
                  pl.BlockSpec((None, nc, LANES), lambda b: (b, 0, 0)),
                  pl.BlockSpec((None, LANES, nc), lambda b: (b, 0, 0)),
                  pl.BlockSpec(pool.shape, lambda b: (0, 0)),
                  pl.BlockSpec(sbias.shape, lambda b: (0, 0, 0))],
        out_specs=[pl.BlockSpec((None, LANES, LANES), lambda b: (b, 0, 0)),
                   pl.BlockSpec((None, LANES, nsel), lambda b: (b, 0, 0))],
        out_shape=[jax.ShapeDtypeStruct((db, LANES, LANES), F32), jax.ShapeDtypeStruct((db, LANES, nsel), BF16)],
        scratch_shapes=[pltpu.VMEM((nc, LANES), F32)],
        compiler_params=_cparams(("arbitrary",)),
        name="sample_nsa_cmp_attn",
    )(bq, kcmp, vcmpt, pool, sbias)


def _s_asm_ab_kernel(lam_ref, gain_ref, oa_ref, oc_ref, os_ref, ow_ref, g_ref, a_out, b_out, *, lam_init):
    lp = lam_ref[...]
    lam = (jnp.exp(jnp.sum(lp[0:1] * lp[1:2], axis=1, keepdims=True))
           - jnp.exp(jnp.sum(lp[2:3] * lp[3:4], axis=1, keepdims=True)) + lam_init)
    for h in range(A_HEADS):
        sl = slice(h * LANES, (h + 1) * LANES)
        o = oa_ref[2 * h * TOK:(2 * h + 1) * TOK, sl] - lam * oa_ref[(2 * h + 1) * TOK:(2 * h + 2) * TOK, sl]
        o = o * lax.rsqrt(jnp.mean(o * o, axis=-1, keepdims=True) + NORM_EPS)
        a_out[:, sl] = o * gain_ref[...] * (1.0 - lam_init)
    lane = lax.broadcasted_iota(I32, (TOK, LANES), 1)
    w = B_HEADS * HEAD_DIM
    for pair in range(B_HEADS // 2):
        acc = jnp.zeros((TOK, LANES), F32)
        for e in range(2):
            h = 2 * pair + e
            kvh = h // B_GROUP
            mine = (lane >= HEAD_DIM) if e else (lane < HEAD_DIM)
            for c, x_ref in enumerate((oc_ref, os_ref, ow_ref)):
                blk = x_ref[h * TOK:(h + 1) * TOK, :]
                if e != kvh:
                    blk = pltpu.roll(blk, HEAD_DIM, 1)
                gate = g_ref[:, c * w + pair * LANES:c * w + (pair + 1) * LANES]
                acc = acc + jnp.where(mine, gate * blk, 0.0)
        b_out[:, pair * LANES:(pair + 1) * LANES] = acc


def _s_assemble_ab(lam_p, gain, oa, oc, os_, ow, gates, lam_init):
    db = oa.shape[0]
    w = B_HEADS * HEAD_DIM
    return pl.pallas_call(
        functools.partial(_s_asm_ab_kernel, lam_init=lam_init),
        grid=(db,),
        in_specs=[pl.BlockSpec(lam_p.shape, lambda b: (0, 0)),
                  pl.BlockSpec((1, LANES), lambda b: (0, 0)),
                  pl.BlockSpec((None, LANES, A_HEADS * LANES), lambda b: (b, 0, 0)),
                  pl.BlockSpec((None, LANES, LANES), lambda b: (b, 0, 0)),
                  pl.BlockSpec((None, LANES, LANES), lambda b: (b, 0, 0)),
                  pl.BlockSpec((None, LANES, LANES), lambda b: (b, 0, 0)),
                  pl.BlockSpec((None, TOK, 3 * w), lambda b: (b, 0, 0))],
        out_specs=[pl.BlockSpec((None, TOK, A_HEADS * LANES), lambda b: (b, 0, 0)),
                   pl.BlockSpec((None, TOK, w), lambda b: (b, 0, 0))],
        out_shape=[jax.ShapeDtypeStruct((db, TOK, A_HEADS * LANES), F32), jax.ShapeDtypeStruct((db, TOK, w), F32)],
        compiler_params=_cparams(("arbitrary",)),
        name="sample_assemble_ab",
    )(lam_p, gain.reshape(1, LANES), oa, oc, os_, ow, gates)


def _s_asm_cd_kernel(oc_ref, od_ref, c_out, d_out):
    w = C_HEADS * HEAD_DIM
    head = lax.broadcasted_iota(I32, (TOK, w), 1) // HEAD_DIM
    for x_ref, o_ref in ((oc_ref, c_out), (od_ref, d_out)):
        out = jnp.zeros((TOK, w), F32)
        for h in range(C_HEADS):
            out = jnp.where(head == h, x_ref[h * TOK:(h + 1) * TOK, :], out)
        o_ref[...] = out


def _s_assemble_cd(oc, od):
    db = oc.shape[0]
    w = C_HEADS * HEAD_DIM
    spec_in = pl.BlockSpec((None, LANES, w), lambda b: (b, 0, 0))
    spec_out = pl.BlockSpec((None, TOK, w), lambda b: (b, 0, 0))
    return pl.pallas_call(
        _s_asm_cd_kernel,
        grid=(db,),
        in_specs=[spec_in, spec_in],
        out_specs=[spec_out, spec_out],
        out_shape=[jax.ShapeDtypeStruct((db, TOK, w), F32)] * 2,
        compiler_params=_cparams(("arbitrary",)),
        name="sample_assemble_cd",
    )(oc, od)


_AB_S_OUTS = (
    ("f32", _AB["a_k"], 1024, None),
    ("f32", _AB["b_kc"], 512, None),
    ("f32", _AB["b_kw"], 256, None),
    ("bf16", _AB["a_q"], 512, SCALE),
    ("silu", _AB["a_gate"], 512, None),
    ("bf16", _AB["b_q"], 512, SCALE),
    ("sigmoid", _ABW["gates"], 3 * B_HEADS * HEAD_DIM, None),
    ("silu", _ABW["b_gate"], 512, None),
)

_CD_S_OUTS = (
    ("f32", _CD["c_k"], 1024, None),
    ("f32", _CD["d_k"], 1024, None),
    ("f32", _CD["c_q"], 512, None),
    ("bf16", _CD["c_q"], 512, SCALE),
    ("silu", _CD["c_gate"], 512, None),
    ("bf16", _CD["d_q"], 512, SCALE),
    ("silu", _CD["d_gate"], 512, None),
)

_VH_A = tuple((0, 0, h * LANES + half * HEAD_DIM, h * LANES + (half + 1) * HEAD_DIM)
              for h in range(A_HEADS) for half in range(2))
_VH_CD = tuple((0, 0, h * HEAD_DIM, (h + 1) * HEAD_DIM) for h in range(C_HEADS))
_VH_B = tuple(((h // 2) * LANES, HEAD_DIM if h % 2 != h // B_GROUP else 0,
               (h // B_GROUP) * HEAD_DIM, (h // B_GROUP + 1) * HEAD_DIM) for h in range(B_HEADS))


def _pad_rows(xs):
    db, dt, d = xs.shape
    return jnp.concatenate([xs, jnp.zeros((db, PAGE - dt, d), xs.dtype)], axis=1).reshape(db * PAGE, d)


def _tok_rows(a, db):
    return a.reshape(db, PAGE, a.shape[-1])[:, :TOK].reshape(db * TOK, a.shape[-1])


def _sample_ab(xs, pt, cache_a, cache_b, cache_win, sb_a, sb_b, norm_g, w_ab, w_out, lam_p, head_gain, cmp_prm, li):
    db, dt = xs.shape[:2]
    npages = pt.shape[1]
    xpad = _pad_rows(xs)
    outs = _project(xpad, norm_g, w_ab, _AB_S_OUTS)
    a_kv, b_kv, b_win, qa, sga, bq, gates, sgb = [a.reshape(db, PAGE, a.shape[-1]) for a in outs]
    lam_init = 0.8 - 0.6 * math.exp(-0.3 * li)
    oa = _s_attention(pt, qa, cache_a, 0, a_kv, 0, vheads=_VH_A, kw=512, vw=512, tok_major=True, bias=sb_a,
                      name="sample_a_attn")
    pos_bd, w1_bd, w2_bd = cmp_prm
    first, second = _s_compress(pt, cache_b, pos_bd, w1_bd)
    kcmp, vcmpt = _b_cmp_finish(first, second, w2_bd)
    nc = npages * (PAGE // CMP_STRIDE)
    cur = npages * PAGE // SEL_BLOCK
    nselp = -(-(cur + 2) // LANES) * LANES
    oc, code = _s_cmp_attention(bq, kcmp, vcmpt, _pool_matrix(nselp, nc), sb_b, _VH_B, cur)
    os_ = _s_attention(pt, bq, cache_b, 1, b_kv, 1, vheads=_VH_B, kw=LANES, vw=LANES, bias=sb_b, code=code,
                       code_blk=SEL_BLOCK, code_new=True, name="sample_nsa_sel_attn")
    nwp = cache_win.shape[3] // PAGE
    ptw = jnp.zeros((db, nwp), I32)
    ow = _s_attention(ptw, bq, cache_win, 0, b_win, 0, vheads=_VH_B, kw=LANES, vw=LANES, bias=sb_b,
                      first_mask=True, pp=nwp, lane_pages=True, name="sample_nsa_win_attn")
    o_a, o_b = _s_assemble_ab(lam_p, head_gain, oa, oc, os_, ow, gates[:, :TOK], lam_init)
    y = _out_project(_tok_rows(xpad, db), o_a.reshape(db * TOK, -1), _tok_rows(sga, db),
                     o_b.reshape(db * TOK, -1), _tok_rows(sgb, db), w_out, norm_g, False)
    return y.reshape(db, TOK, D_MODEL)[:, :dt], a_kv[:, :dt], b_kv[:, :dt], b_win[:, :dt]


def _sample_cd(xs, pt, cache_c, cache_d, sb_c, norm_g, w_cd, w_out, final_g, final):
    db, dt = xs.shape[:2]
    xpad = _pad_rows(xs)
    outs = _project(xpad, norm_g, w_cd, _CD_S_OUTS)
    c_kv, d_kv, cq, cqb, sgc, qd, sgd = [a.reshape(db, PAGE, a.shape[-1]) for a in outs]
    code = _s_moba_gate(pt, cq, cache_c, _VH_CD)
    oc = _s_attention(pt, cqb, cache_c, 0, c_kv, 0, vheads=_VH_CD, kw=512, vw=512, bias=sb_c, code=code,
                      code_blk=MOBA_BLOCK, name="sample_moba_attn")
    od = _s_attention(pt, qd, cache_d, 0, d_kv, 0, vheads=_VH_CD, kw=512, vw=512, stick=True, name="sample_d_attn")
    o_c, o_d = _s_assemble_cd(oc, od)
    y = _out_project(_tok_rows(xpad, db), o_c.reshape(db * TOK, -1), _tok_rows(sgc, db),
                     o_d.reshape(db * TOK, -1), _tok_rows(sgd, db), w_out, final_g, final)
    return y.reshape(db, TOK, D_MODEL)[:, :dt], c_kv[:, :dt], d_kv[:, :dt]


def kernel(x_prompt, x_sample, cache_a_kv, cache_b_kv, cache_b_win, cache_c_kv, cache_d_kv, page_table,
           rel_bias, ab_norm, ab_w_in, ab_w_out, ab_lambda, ab_head_norm, ab_cmp_pos, ab_cmp_w1, ab_cmp_w2,
           cd_norm, cd_w_in, cd_w_out, final_norm):
    t = x_prompt.shape[1]
    db, dt = x_sample.shape[:2]
    assert x_prompt.shape[0] == 1 and dt <= TOK and ab_norm.shape[0] == 1 and cd_norm.shape[0] == 1
    n_phys = cache_a_kv.shape[1]
    tq = min(TQ, t)
    bias = _bias_tiles(rel_bias, tq, tq)
    cbias = _cmp_bias_tiles(rel_bias, 2 * (tq // CMP_STRIDE), tq)
    w_ab = _ab_weight(ab_w_in[0])
    w_cd = cd_w_in[0].astype(BF16)
    cmp_prm = _cmp_params(ab_cmp_pos[0], ab_cmp_w1[0], ab_cmp_w2[0])

    xp, a_kv, b_kv, b_win = _prompt_ab(x_prompt[0], bias, cbias, ab_norm[0], w_ab, ab_w_out[0],
                                       ab_lambda[0], ab_head_norm[0], cmp_prm, 0)
    yp, c_kv, d_kv = _prompt_cd(xp, bias, cd_norm[0], w_cd, cd_w_out[0], final_norm, True)

    sb_a = _s_bias(rel_bias, [h for h in range(A_HEADS) for _ in range(2)])
    sb_b = _s_bias(rel_bias, [A_HEADS + h for h in range(B_HEADS)])
    sb_c = _s_bias(rel_bias, list(range(C_HEADS)))
    cache_a = cache_a_kv[0].reshape(n_phys, PAGE * 2 * A_HEADS, 2 * HEAD_DIM)
    cache_b = jnp.transpose(cache_b_kv[0], (0, 2, 3, 4, 1)).reshape(n_phys, 4, B_KV_HEADS * HEAD_DIM, PAGE)
    cache_c = jnp.transpose(cache_c_kv[0], (0, 2, 3, 4, 1)).reshape(n_phys, 2, C_HEADS * HEAD_DIM, PAGE)
    cache_d = jnp.transpose(cache_d_kv[0], (0, 2, 3, 4, 1)).reshape(n_phys, 2, D_HEADS * HEAD_DIM, PAGE)
    wbuf = cache_b_win.shape[2]
    cache_win = jnp.transpose(cache_b_win[0], (0, 2, 3, 4, 1)).reshape(db, 2, B_KV_HEADS * HEAD_DIM, wbuf)
    xs, a_s, b_s, w_s = _sample_ab(x_sample, page_table, cache_a, cache_b, cache_win, sb_a, sb_b, ab_norm[0], w_ab,
                                   ab_w_out[0], ab_lambda[0], ab_head_norm[0], cmp_prm, 0)
    ys, c_s, d_s = _sample_cd(xs, page_table, cache_c, cache_d, sb_c, cd_norm[0], w_cd, cd_w_out[0], final_norm, True)

    keep = min(WINDOW, t)
    win_s = jnp.concatenate([cache_b_win[0].reshape(db, wbuf, -1), w_s], axis=1)[:, -min(WINDOW, wbuf + dt):]
    return (yp[None], ys,
            a_kv.reshape(1, 1, t, 2, A_HEADS, 2 * HEAD_DIM), a_s.reshape(1, db, dt, 2, A_HEADS, 2 * HEAD_DIM),
            b_kv.reshape(1, 1, t, 4, B_KV_HEADS, HEAD_DIM), b_s.reshape(1, db, dt, 4, B_KV_HEADS, HEAD_DIM),
            b_win[t - keep:].reshape(1, 1, keep, 2, B_KV_HEADS, HEAD_DIM),
            win_s.reshape(1, db, -1, 2, B_KV_HEADS, HEAD_DIM),
            c_kv.reshape(1, 1, t, 2, C_HEADS, HEAD_DIM), c_s.reshape(1, db, dt, 2, C_HEADS, HEAD_DIM),
            d_kv.reshape(1, 1, t, 2, D_HEADS, HEAD_DIM), d_s.reshape(1, db, dt, 2, D_HEADS, HEAD_DIM))
```
